```python
import math
import jax
import jax.numpy as jnp
from jax import lax
import numpy as np

D_MODEL = 4096
BATCH = 2
SEQ = 8192
DEPTH = 4

HEAD_DIM = 128
N_BRANCH = 4
BRANCH_W = D_MODEL // N_BRANCH
MIX_W = N_BRANCH * BRANCH_W

A_HEADS = BRANCH_W // HEAD_DIM
IDX_HEADS = 16
IDX_DIM = 64
IDX_W_SCALE = (IDX_HEADS * IDX_DIM) ** -0.5
TOPK_MAX = 256
Q_BLOCK = 128
B_HEADS = BRANCH_W // HEAD_DIM
B_CONV = 4
B_CHUNK = 64
C_CHUNK = 128
C_GROUPS = 8
C_GROUP_DIM = BRANCH_W // C_GROUPS
D_CONV = 3
GATE_RANK = 256
D_FF = 4 * D_MODEL
EPS = 1e-6

IN_SIZES = (
    A_HEADS * HEAD_DIM,
    HEAD_DIM,
    HEAD_DIM,
    IDX_HEADS * IDX_DIM,
    IDX_DIM,
    IDX_HEADS,
    3 * BRANCH_W,
    BRANCH_W,
    B_HEADS,
    B_HEADS,
    2 * BRANCH_W,
    3 * BRANCH_W,
    GATE_RANK,
)
N_IN = sum(IN_SIZES)

kernel_name = 'hybrid_dsa_gdn_gmlp_shortconv_block'


def rmsnorm(x, gain):
    xf = x.astype(jnp.float32)
    y = xf * lax.rsqrt(jnp.mean(xf * xf, axis=-1, keepdims=True) + EPS)
    return (y * gain.astype(jnp.float32)).astype(x.dtype)


def layernorm(x, gain, bias):
    xf = x.astype(jnp.float32)
    xc = xf - jnp.mean(xf, axis=-1, keepdims=True)
    y = xc * lax.rsqrt(jnp.mean(xc * xc, axis=-1, keepdims=True) + EPS)
    return (y * gain.astype(jnp.float32) + bias.astype(jnp.float32)).astype(x.dtype)


def l2norm(x):
    return x * lax.rsqrt(jnp.sum(x * x, axis=-1, keepdims=True) + EPS)


def causal_dwconv(x, w):
    width, ch = w.shape
    return lax.conv_general_dilated(
        x, w[:, None, :].astype(x.dtype), window_strides=(1,), padding=[(width - 1, 0)],
        dimension_numbers=('NWC', 'WIO', 'NWC'), feature_group_count=ch)


def split_cols(p, sizes):
    parts, start = [], 0
    for s in sizes:
        parts.append(p[..., start:start + s])
        start += s
    return parts


def dsa_attention(q, k, v, q_idx, k_idx, w_idx):
    bn, L = q.shape[0], q.shape[1]
    topk = min(TOPK_MAX, L // 4)
    n_blk = L // Q_BLOCK
    scale = HEAD_DIM ** -0.5
    key_pos = jnp.arange(L)

    def block(i):
        s0 = i * Q_BLOCK
        qb = lax.dynamic_slice_in_dim(q, s0, Q_BLOCK, axis=1)
        qib = lax.dynamic_slice_in_dim(q_idx, s0, Q_BLOCK, axis=1)
        wb = lax.dynamic_slice_in_dim(w_idx, s0, Q_BLOCK, axis=1)
        q_pos = s0 + jnp.arange(Q_BLOCK)
        rel = jax.nn.relu(jnp.einsum('bqhd,bsd->bqhs', qib, k_idx).astype(jnp.float32))
        score = jnp.einsum('bqh,bqhs->bqs', wb.astype(jnp.float32), rel)
        causal = key_pos[None, :] <= q_pos[:, None]
        score = jnp.where(causal[None], score, -jnp.inf)
        _, idx = lax.top_k(score, topk)
        valid = idx <= q_pos[None, :, None]
        kg = jax.vmap(lambda kb, ib: kb[ib])(k, idx)
        vg = jax.vmap(lambda vb, ib: vb[ib])(v, idx)
        logits = jnp.einsum('bqhd,bqkd->bqhk', qb, kg).astype(jnp.float32) * scale
        logits = jnp.where(valid[:, :, None, :], logits, -jnp.inf)
        p = jax.nn.softmax(logits, axis=-1).astype(v.dtype)
        return jnp.einsum('bqhk,bqkd->bqhd', p, vg)

    out = lax.map(block, jnp.arange(n_blk))
    return jnp.moveaxis(out, 0, 1).reshape(bn, L, A_HEADS * HEAD_DIM)


def chunk_gated_delta_rule(q, k, v, g, beta):
    bn, L, H, dk = q.shape
    dv = v.shape[-1]
    n = L // B_CHUNK

    def to_chunks(t):
        t = t.reshape((bn, n, B_CHUNK, H) + t.shape[3:])
        return jnp.moveaxis(t, 3, 1)

    q = to_chunks(q * dk ** -0.5)
    k, v, g, beta = to_chunks(k), to_chunks(v), to_chunks(g), to_chunks(beta)
    g = jnp.cumsum(g, axis=-1)
    k_beta = k * beta[..., None]
    v_beta = v * beta[..., None]
    incl = jnp.tril(jnp.ones((B_CHUNK, B_CHUNK), dtype=bool))
    strict = jnp.tril(jnp.ones((B_CHUNK, B_CHUNK), dtype=bool), -1)
    decay = jnp.exp(jnp.where(incl, g[..., :, None] - g[..., None, :], -jnp.inf))
    lower = jnp.where(strict, jnp.einsum('bhnid,bhnjd->bhnij', k_beta, k) * decay, 0.0)
    eye = jnp.eye(B_CHUNK, dtype=q.dtype)
    t_inv = lax.linalg.triangular_solve(eye + lower, jnp.broadcast_to(eye, lower.shape),
                                        left_side=True, lower=True, unit_diagonal=True)
    u = t_inv @ v_beta
    w = t_inv @ (k_beta * jnp.exp(g)[..., None])
    attn = jnp.where(incl, jnp.einsum('bhnid,bhnjd->bhnij', q, k) * decay, 0.0)
    q_dec = q * jnp.exp(g)[..., None]
    k_dec = k * jnp.exp(g[..., -1:] - g)[..., None]
    chunk_decay = jnp.exp(g[..., -1])

    def step(state, xs):
        q_c, k_c, u_c, w_c, a_c, d_c = xs
        v_new = u_c - w_c @ state
        out = q_c @ state + a_c @ v_new
        state = state * d_c[..., None, None] + jnp.swapaxes(k_c, -1, -2) @ v_new
        return state, out

    xs = tuple(jnp.moveaxis(t, 2, 0) for t in (q_dec, k_dec, u, w, attn, chunk_decay))
    state0 = jnp.zeros((bn, H, dk, dv), q.dtype)
    _, out = lax.scan(step, state0, xs)
    return jnp.transpose(out, (1, 0, 3, 2, 4)).reshape(bn, L, H, dv)


def gated_deltanet(qkv, z, a, b, conv_w, a_log, dt_bias, out_gain):
    bn, L, _ = qkv.shape
    qkv = jax.nn.silu(causal_dwconv(qkv, conv_w))
    q, k, v = jnp.split(qkv, 3, axis=-1)

    def heads(t):
        return t.reshape(bn, L, B_HEADS, HEAD_DIM).astype(jnp.float32)

    q, k, v = l2norm(heads(q)), l2norm(heads(k)), heads(v)
    beta = jax.nn.sigmoid(b.astype(jnp.float32))
    g = -jnp.exp(a_log.astype(jnp.float32)) * jax.nn.softplus(a.astype(jnp.float32) + dt_bias.astype(jnp.float32))
    o = chunk_gated_delta_rule(q, k, v, g, beta).astype(z.dtype)
    o = rmsnorm(o, out_gain) * jax.nn.silu(z.reshape(bn, L, B_HEADS, HEAD_DIM))
    return o.reshape(bn, L, B_HEADS * HEAD_DIM)


def spatial_gating_mlp(uv, ln_gain, ln_bias, w_s, b_s):
    bn, L, _ = uv.shape
    n = L // C_CHUNK
    uv = jax.nn.gelu(uv)
    u, v = uv[..., :BRANCH_W], uv[..., BRANCH_W:]
    v = layernorm(v.reshape(bn, n, C_CHUNK, C_GROUPS, C_GROUP_DIM),
                  ln_gain.reshape(C_GROUPS, C_GROUP_DIM), ln_bias.reshape(C_GROUPS, C_GROUP_DIM))
    causal = jnp.tril(jnp.ones((C_CHUNK, C_CHUNK), dtype=bool))
    w_m = jnp.where(causal, w_s, 0).astype(v.dtype)
    s = jnp.einsum('gts,bnsgc->bntgc', w_m, v) + b_s.T[:, :, None].astype(v.dtype)
    return u * s.reshape(bn, L, BRANCH_W)


def short_conv_mixer(hbc, conv_w):
    h, gate_b, gate_c = jnp.split(hbc, 3, axis=-1)
    return gate_b * causal_dwconv(gate_c * h, conv_w)


def setup_inputs(seed: int = 0) -> dict:
    key = jax.random.key(seed)
    ks = jax.random.split(key, 24)
    f32 = jnp.float32

    def normal(k, shape, scale):
        return jax.random.normal(k, shape, f32) * scale

    def gain(k, shape):
        return 1.0 + 0.02 * jax.random.normal(k, shape, f32)

    dt = jnp.exp(jax.random.uniform(ks[8], (DEPTH, B_HEADS), f32, math.log(1e-3), math.log(1e-1)))
    return {
        'x': normal(ks[0], (BATCH, SEQ, D_MODEL), 1.0),
        'mix_norm': gain(ks[1], (DEPTH, D_MODEL)),
        'w_in': normal(ks[2], (DEPTH, D_MODEL, N_IN), D_MODEL ** -0.5),
        'a_q_norm': gain(ks[3], (DEPTH, HEAD_DIM)),
        'a_k_norm': gain(ks[4], (DEPTH, HEAD_DIM)),
        'a_idx_k_norm': gain(ks[5], (DEPTH, IDX_DIM)),
        'b_conv': normal(ks[6], (DEPTH, B_CONV, 3 * BRANCH_W), B_CONV ** -0.5),
        'b_a_log': jnp.log(jax.random.uniform(ks[7], (DEPTH, B_HEADS), f32, 1.0, 16.0)),
        'b_dt_bias': dt + jnp.log(-jnp.expm1(-dt)),
        'b_out_norm': gain(ks[9], (DEPTH, HEAD_DIM)),
        'c_ln_gain': gain(ks[10], (DEPTH, BRANCH_W)),
        'c_ln_bias': normal(ks[11], (DEPTH, BRANCH_W), 0.02),
        'c_spatial_w': normal(ks[12], (DEPTH, C_GROUPS, C_CHUNK, C_CHUNK), C_CHUNK ** -0.5),
        'c_spatial_b': gain(ks[13], (DEPTH, C_GROUPS, C_CHUNK)),
        'd_conv': normal(ks[14], (DEPTH, D_CONV, BRANCH_W), D_CONV ** -0.5),
        'w_branch': normal(ks[15], (DEPTH, MIX_W, D_MODEL), BRANCH_W ** -0.5),
        'w_gate_up': normal(ks[16], (DEPTH, N_BRANCH, GATE_RANK, D_MODEL), GATE_RANK ** -0.5),
        'w_out': normal(ks[17], (DEPTH, D_MODEL, D_MODEL), D_MODEL ** -0.5),
        'ffn_norm': gain(ks[18], (DEPTH, D_MODEL)),
        'w_ff1': normal(ks[19], (DEPTH, D_MODEL, D_FF), D_MODEL ** -0.5),
        'w_ff2': normal(ks[20], (DEPTH, D_FF, D_MODEL), D_FF ** -0.5),
    }


def reference(x, mix_norm, w_in, a_q_norm, a_k_norm, a_idx_k_norm, b_conv, b_a_log, b_dt_bias,
              b_out_norm, c_ln_gain, c_ln_bias, c_spatial_w, c_spatial_b, d_conv, w_branch,
              w_gate_up, w_out, ffn_norm, w_ff1, w_ff2):
    bn, L, _ = x.shape
    for l in range(DEPTH):
        h = rmsnorm(x, mix_norm[l])
        (a_q, a_k, a_v, i_q, i_k, i_w, b_qkv, b_z, b_a, b_b, c_uv, d_hbc, g_lat) = split_cols(h @ w_in[l], IN_SIZES)
        q = rmsnorm(a_q.reshape(bn, L, A_HEADS, HEAD_DIM), a_q_norm[l])
        k = rmsnorm(a_k, a_k_norm[l])
        o_a = dsa_attention(q, k, a_v, i_q.reshape(bn, L, IDX_HEADS, IDX_DIM),
                            rmsnorm(i_k, a_idx_k_norm[l]), i_w * IDX_W_SCALE)
        o_b = gated_deltanet(b_qkv, b_z, b_a, b_b, b_conv[l], b_a_log[l], b_dt_bias[l], b_out_norm[l])
        o_c = spatial_gating_mlp(c_uv, c_ln_gain[l], c_ln_bias[l], c_spatial_w[l], c_spatial_b[l])
        o_d = short_conv_mixer(d_hbc, d_conv[l])
        merged = jnp.zeros_like(x)
        for i, o in enumerate((o_a, o_b, o_c, o_d)):
            y = o @ w_branch[l, i * BRANCH_W:(i + 1) * BRANCH_W]
            merged = merged + jax.nn.sigmoid(g_lat @ w_gate_up[l, i]) * y
        x = x + merged @ w_out[l]
        f = rmsnorm(x, ffn_norm[l])
        x = x + jnp.square(jax.nn.relu(f @ w_ff1[l])) @ w_ff2[l]
    return x
```

```python
import functools

import jax
import jax.numpy as jnp
from jax import lax
from jax.experimental import pallas as pl
from jax.experimental.pallas import tpu as pltpu

D_MODEL = 4096
HEAD_DIM = 128
BRANCH_W = 1024
A_HEADS = 8
IDX_HEADS = 16
IDX_DIM = 64
IDX_W_SCALE = (IDX_HEADS * IDX_DIM) ** -0.5
TOPK_MAX = 256
Q_BLOCK = 128
B_HEADS = 8
B_CONV = 4
B_CHUNK = 64
C_CHUNK = 128
C_GROUPS = 8
D_CONV = 3
GATE_RANK = 256
EPS = 1e-6

P_AQ, P_IQ, P_BQ, P_BZ, P_CU, P_CV, P_DH, P_DB, P_DC = 0, 1024, 2048, 5120, 6144, 7168, 8192, 9216, 10240
P_G, P_AK, P_AV, P_SM = 11264, 11520, 11648, 11776
N_P = 11904
SM_IK, SM_IW, SM_BA, SM_BB = 0, 64, 80, 88

MXU_DTYPE = jnp.bfloat16
HALO = 8
VMEM_LIMIT = 56 * 1024 * 1024
NEG_BIG = -1e30
INT_MIN = -(2 ** 31)
KEY_NEG_INF = INT_MIN + 0x7FFFFF


def _cparams(*sem):
    return pltpu.CompilerParams(dimension_semantics=sem, vmem_limit_bytes=VMEM_LIMIT)


def _nt_dot(a, b, precision=None):
    return lax.dot_general(a, b, (((1,), (1,)), ((), ())), precision=precision,
                           preferred_element_type=jnp.float32)


def _dot(a, b, precision=None):
    return jnp.dot(a, b, precision=precision, preferred_element_type=jnp.float32)


def _rmsnorm_kernel(x_ref, g_ref, o_ref):
    x = x_ref[...]
    ms = jnp.mean(x * x, axis=-1, keepdims=True)
    o_ref[...] = (x * lax.rsqrt(ms + EPS) * g_ref[...]).astype(o_ref.dtype)


def rmsnorm_rows(x, gain, tl=256):
    t, d = x.shape
    return pl.pallas_call(
        _rmsnorm_kernel,
        grid=(t // tl,),
        in_specs=[pl.BlockSpec((tl, d), lambda i: (i, 0)),
                  pl.BlockSpec((1, d), lambda i: (0, 0))],
        out_specs=pl.BlockSpec((tl, d), lambda i: (i, 0)),
        out_shape=jax.ShapeDtypeStruct((t, d), MXU_DTYPE),
        compiler_params=_cparams("parallel"),
        name="rmsnorm",
    )(x, gain.reshape(1, d))


def _mm_kernel(a_ref, w_ref, *rest, nk, epilogue):
    if epilogue == "residual":
        r_ref, o_ref = rest[0], rest[1]
        scratch = rest[2:]
    else:
        r_ref, o_ref = None, rest[0]
        scratch = rest[1:]
    part = _dot(a_ref[...], w_ref[...])

    def finish(acc):
        if epilogue == "residual":
            o_ref[...] = r_ref[...] + acc
        elif epilogue == "relu2":
            o_ref[...] = jnp.square(jnp.maximum(acc, 0.0)).astype(o_ref.dtype)
        else:
            o_ref[...] = acc.astype(o_ref.dtype)

    if nk == 1:
        finish(part)
    else:
        acc_ref = scratch[0]
        k = pl.program_id(2)

        @pl.when(k == 0)
        def _():
            acc_ref[...] = part

        @pl.when(k > 0)
        def _():
            acc_ref[...] += part

        @pl.when(k == nk - 1)
        def _():
            finish(acc_ref[...])


def matmul(a, w, *, epilogue="none", residual=None, out_dtype=jnp.float32, tm=1024, tn=1024, tk=4096):
    m, kdim = a.shape
    n = w.shape[1]
    tm, tn, tk = min(tm, m), min(tn, n), min(tk, kdim)
    nk = kdim // tk
    in_specs = [pl.BlockSpec((tm, tk), lambda i, j, k: (i, k)),
                pl.BlockSpec((tk, tn), lambda i, j, k: (k, j))]
    args = [a, w]
    if epilogue == "residual":
        in_specs.append(pl.BlockSpec((tm, tn), lambda i, j, k: (i, j)))
        args.append(residual)
    scratch = [pltpu.VMEM((tm, tn), jnp.float32)] if nk > 1 else []
    return pl.pallas_call(
        functools.partial(_mm_kernel, nk=nk, epilogue=epilogue),
        grid=(m // tm, pl.cdiv(n, tn), nk),
        in_specs=in_specs,
        out_specs=pl.BlockSpec((tm, tn), lambda i, j, k: (i, j)),
        out_shape=jax.ShapeDtypeStruct((m, n), out_dtype),
        scratch_shapes=scratch,
        compiler_params=_cparams("parallel", "parallel", "arbitrary"),
        name="matmul_" + epilogue,
    )(*args)


def _dconv_kernel(h_ref, b_ref, c_ref, hp_ref, cp_ref, w_ref, o_ref, ext_ref, *, tiles_per_seq):
    tl = h_ref.shape[0]
    first = (pl.program_id(0) % tiles_per_seq) == 0
    ext_ref[0:HALO, :] = jnp.where(first, 0.0, cp_ref[...] * hp_ref[...])
    p = c_ref[...] * h_ref[...]
    ext_ref[HALO:HALO + tl, :] = p
    w = w_ref[...]
    acc = w[2:3, :] * p
    acc = acc + w[1:2, :] * ext_ref[HALO - 1:HALO - 1 + tl, :]
    acc = acc + w[0:1, :] * ext_ref[HALO - 2:HALO - 2 + tl, :]
    o_ref[...] = (b_ref[...] * acc).astype(o_ref.dtype)


def short_conv_mixer(proj, conv_w, seq_len, tl=512):
    t = proj.shape[0]
    w = BRANCH_W
    cur = lambda cb: pl.BlockSpec((tl, w), lambda i: (i, cb))
    prev = lambda cb: pl.BlockSpec((HALO, w), lambda i: (jnp.maximum(i * (tl // HALO) - 1, 0), cb))
    return pl.pallas_call(
        functools.partial(_dconv_kernel, tiles_per_seq=seq_len // tl),
        grid=(t // tl,),
        in_specs=[cur(P_DH // w), cur(P_DB // w), cur(P_DC // w), prev(P_DH // w), prev(P_DC // w),
                  pl.BlockSpec((D_CONV, w), lambda i: (0, 0))],
        out_specs=pl.BlockSpec((tl, w), lambda i: (i, 0)),
        out_shape=jax.ShapeDtypeStruct((t, w), MXU_DTYPE),
        scratch_shapes=[pltpu.VMEM((tl + HALO, w), jnp.float32)],
        compiler_params=_cparams("parallel"),
        name="mixer_d",
    )(proj, proj, proj, proj, proj, conv_w)


def _gmlp_kernel(u_ref, v_ref, lg_ref, lb_ref, ws_ref, bs_ref, o_ref):
    tl = u_ref.shape[0]
    gd = BRANCH_W // C_GROUPS
    row = lax.broadcasted_iota(jnp.int32, (C_CHUNK, C_CHUNK), 0)
    col = lax.broadcasted_iota(jnp.int32, (C_CHUNK, C_CHUNK), 1)
    causal = row >= col
    for g in range(C_GROUPS):
        w_m = jnp.where(causal, ws_ref[g], 0.0).astype(MXU_DTYPE)
        bias = bs_ref[:, g:g + 1]
        gain = lg_ref[:, g * gd:(g + 1) * gd]
        shift = lb_ref[:, g * gd:(g + 1) * gd]
        for c in range(tl // C_CHUNK):
            rows = slice(c * C_CHUNK, (c + 1) * C_CHUNK)
            cols = slice(g * gd, (g + 1) * gd)
            v = jax.nn.gelu(v_ref[rows, cols])
            mu = jnp.mean(v, axis=-1, keepdims=True)
            vc = v - mu
            var = jnp.mean(vc * vc, axis=-1, keepdims=True)
            vn = vc * lax.rsqrt(var + EPS) * gain + shift
            s = _dot(w_m, vn.astype(MXU_DTYPE)) + bias
            o_ref[rows, cols] = (jax.nn.gelu(u_ref[rows, cols]) * s).astype(o_ref.dtype)


def spatial_gating_mixer(proj, ln_gain, ln_bias, w_s, b_s, tl=256):
    t = proj.shape[0]
    w = BRANCH_W
    return pl.pallas_call(
        _gmlp_kernel,
        grid=(t // tl,),
        in_specs=[pl.BlockSpec((tl, w), lambda i: (i, P_CU // w)),
                  pl.BlockSpec((tl, w), lambda i: (i, P_CV // w)),
                  pl.BlockSpec((1, w), lambda i: (0, 0)),
                  pl.BlockSpec((1, w), lambda i: (0, 0)),
                  pl.BlockSpec((C_GROUPS, C_CHUNK, C_CHUNK), lambda i: (0, 0, 0)),
                  pl.BlockSpec((C_CHUNK, C_GROUPS), lambda i: (0, 0))],
        out_specs=pl.BlockSpec((tl, w), lambda i: (i, 0)),
        out_shape=jax.ShapeDtypeStruct((t, w), MXU_DTYPE),
        compiler_params=_cparams("parallel"),
        name="mixer_c",
    )(proj, proj, ln_gain.reshape(1, w), ln_bias.reshape(1, w), w_s, b_s.T)


def _a_prep_kernel(k_ref, v_ref, sm_ref, gk_ref, gi_ref, kn_ref, vb_ref, ki_ref):
    k = k_ref[...]
    kn = k * lax.rsqrt(jnp.mean(k * k, axis=-1, keepdims=True) + EPS) * gk_ref[...]
    kn_ref[...] = kn.astype(kn_ref.dtype)
    vb_ref[...] = v_ref[...].astype(vb_ref.dtype)
    ik = sm_ref[:, SM_IK:SM_IK + IDX_DIM]
    ikn = ik * lax.rsqrt(jnp.mean(ik * ik, axis=-1, keepdims=True) + EPS) * gi_ref[...]
    ki_ref[...] = ikn.astype(ki_ref.dtype)


def dsa_prep(proj, k_gain, ik_gain, tl=512):
    t = proj.shape[0]
    hd = HEAD_DIM
    return pl.pallas_call(
        _a_prep_kernel,
        grid=(t // tl,),
        in_specs=[pl.BlockSpec((tl, hd), lambda i: (i, P_AK // hd)),
                  pl.BlockSpec((tl, hd), lambda i: (i, P_AV // hd)),
                  pl.BlockSpec((tl, hd), lambda i: (i, P_SM // hd)),
                  pl.BlockSpec((1, hd), lambda i: (0, 0)),
                  pl.BlockSpec((1, IDX_DIM), lambda i: (0, 0))],
        out_specs=[pl.BlockSpec((tl, hd), lambda i: (i, 0)),
                   pl.BlockSpec((tl, hd), lambda i: (i, 0)),
                   pl.BlockSpec((tl, IDX_DIM), lambda i: (i, 0))],
        out_shape=[jax.ShapeDtypeStruct((t, hd), MXU_DTYPE),
                   jax.ShapeDtypeStruct((t, hd), MXU_DTYPE),
                   jax.ShapeDtypeStruct((t, IDX_DIM), MXU_DTYPE)],
        compiler_params=_cparams("parallel"),
        name="mixer_a_prep",
    )(proj, proj, proj, k_gain.reshape(1, hd), ik_gain.reshape(1, IDX_DIM))


def _dsa_kernel(q_ref, iq_ref, sm_ref, kn_ref, vb_ref, ki_ref, gq_ref, o_ref,
                keys_ref, qs_ref, qi_ref, m_ref, l_ref, acc_ref, *, topk, ck):
    qb = Q_BLOCK
    i = pl.program_id(1)
    n_chunks = (i * qb + qb + ck - 1) // ck
    lanes = 128

    gq = gq_ref[...]
    for h in range(A_HEADS):
        qh = q_ref[:, h * HEAD_DIM:(h + 1) * HEAD_DIM]
        qn = qh * lax.rsqrt(jnp.mean(qh * qh, axis=-1, keepdims=True) + EPS) * gq
        qs_ref[h * qb:(h + 1) * qb, :] = qn.astype(qs_ref.dtype)
    for h in range(IDX_HEADS):
        qi_ref[h] = iq_ref[:, h * IDX_DIM:(h + 1) * IDX_DIM].astype(qi_ref.dtype)
    w_idx = sm_ref[:, SM_IW:SM_IW + IDX_HEADS] * IDX_W_SCALE

    q_pos = i * qb + lax.broadcasted_iota(jnp.int32, (qb, ck), 0)
    k_off = lax.broadcasted_iota(jnp.int32, (qb, ck), 1)

    def score_chunk(c, carry):
        start = pl.multiple_of(c * ck, ck)
        kc = ki_ref[pl.ds(start, ck), :]
        acc = jnp.zeros((qb, ck), jnp.float32)
        for h in range(IDX_HEADS):
            r = _nt_dot(qi_ref[h], kc)
            acc = acc + w_idx[:, h:h + 1] * jnp.maximum(r, 0.0)
        score = jnp.where(k_off + start <= q_pos, acc, -jnp.inf)
        bits = pltpu.bitcast(score, jnp.int32)
        keys_ref[:, pl.ds(start, ck)] = bits ^ ((bits >> 31) & 0x7FFFFFFF)
        return carry

    lax.fori_loop(0, n_chunks, score_chunk, 0)

    def bit_step(b, tau_u):
        cand_u = tau_u | lax.shift_left(jnp.int32(1), 31 - b)
        cand_s = cand_u ^ INT_MIN

        def count_chunk(c, cnt):
            start = pl.multiple_of(c * ck, ck)
            kk = keys_ref[:, pl.ds(start, ck)]
            for g in range(ck // lanes):
                cnt = cnt + jnp.where(kk[:, g * lanes:(g + 1) * lanes] >= cand_s, 1.0, 0.0)
            return cnt

        cnt = lax.fori_loop(0, n_chunks, count_chunk, jnp.zeros((qb, lanes), jnp.float32))
        total = jnp.sum(cnt, axis=1, keepdims=True)
        return jnp.where(total >= float(topk), cand_u, tau_u)

    tau_u = lax.fori_loop(0, 32, bit_step, jnp.zeros((qb, lanes), jnp.int32))
    tau = jnp.maximum(tau_u ^ INT_MIN, KEY_NEG_INF + 1)

    m_ref[...] = jnp.full(m_ref.shape, NEG_BIG, jnp.float32)
    l_ref[...] = jnp.zeros(l_ref.shape, jnp.float32)
    acc_ref[...] = jnp.zeros(acc_ref.shape, jnp.float32)
    scale = HEAD_DIM ** -0.5

    def attn_chunk(c, carry):
        start = pl.multiple_of(c * ck, ck)
        kc = kn_ref[pl.ds(start, ck), :]
        vc = vb_ref[pl.ds(start, ck), :]
        kk = keys_ref[:, pl.ds(start, ck)]
        sel = jnp.concatenate([kk[:, g * lanes:(g + 1) * lanes] >= tau for g in range(ck // lanes)], axis=1)
        s = _nt_dot(qs_ref[...], kc) * scale
        s = jnp.where(sel[None], s.reshape(A_HEADS, qb, ck), NEG_BIG).reshape(A_HEADS * qb, ck)
        m_old = m_ref[...]
        m_new = jnp.maximum(m_old, jnp.max(s, axis=-1, keepdims=True))
        alpha = jnp.exp(m_old - m_new)
        p = jnp.exp(s - m_new)
        l_ref[...] = alpha * l_ref[...] + jnp.sum(p, axis=-1, keepdims=True)
        acc_ref[...] = alpha * acc_ref[...] + _dot(p.astype(MXU_DTYPE), vc)
        m_ref[...] = m_new
        return carry

    lax.fori_loop(0, n_chunks, attn_chunk, 0)

    inv_l = 1.0 / l_ref[...]
    for h in range(A_HEADS):
        rows = slice(h * qb, (h + 1) * qb)
        o_ref[:, h * HEAD_DIM:(h + 1) * HEAD_DIM] = (acc_ref[rows, :] * inv_l[rows, :]).astype(o_ref.dtype)


def dsa_mixer(proj, kn, vb, ki, q_gain, batch, seq_len, ck=512):
    t = proj.shape[0]
    qb = Q_BLOCK
    nq = seq_len // qb
    ck = min(ck, seq_len)
    topk = min(TOPK_MAX, seq_len // 4)
    w = BRANCH_W
    return pl.pallas_call(
        functools.partial(_dsa_kernel, topk=topk, ck=ck),
        grid=(batch, nq),
        in_specs=[pl.BlockSpec((qb, w), lambda b, i: (b * nq + i, P_AQ // w)),
                  pl.BlockSpec((qb, w), lambda b, i: (b * nq + i, P_IQ // w)),
                  pl.BlockSpec((qb, HEAD_DIM), lambda b, i: (b * nq + i, P_SM // HEAD_DIM)),
                  pl.BlockSpec((seq_len, HEAD_DIM), lambda b, i: (b, 0)),
                  pl.BlockSpec((seq_len, HEAD_DIM), lambda b, i: (b, 0)),
                  pl.BlockSpec((seq_len, IDX_DIM), lambda b, i: (b, 0)),
                  pl.BlockSpec((1, HEAD_DIM), lambda b, i: (0, 0))],
        out_specs=pl.BlockSpec((qb, w), lambda b, i: (b * nq + i, 0)),
        out_shape=jax.ShapeDtypeStruct((t, w), MXU_DTYPE),
        scratch_shapes=[pltpu.VMEM((qb, seq_len), jnp.int32),
                        pltpu.VMEM((A_HEADS * qb, HEAD_DIM), MXU_DTYPE),
                        pltpu.VMEM((IDX_HEADS, qb, IDX_DIM), MXU_DTYPE),
                        pltpu.VMEM((A_HEADS * qb, 1), jnp.float32),
                        pltpu.VMEM((A_HEADS * qb, 1), jnp.float32),
                        pltpu.VMEM((A_HEADS * qb, HEAD_DIM), jnp.float32)],
        compiler_params=_cparams("parallel", "arbitrary"),
        name="mixer_a",
    )(proj, proj, proj, kn, vb, ki, q_gain.reshape(1, HEAD_DIM))


def _b_prep_kernel(q_ref, k_ref, v_ref, qp_ref, kp_ref, vp_ref, sm_ref, cw_ref, alog_ref, dtb_ref,
                   qo_ref, ko_ref, vo_ref, g_ref, beta_ref, ext_ref, *, tiles_per_seq):
    tl = q_ref.shape[0]
    w = BRANCH_W
    first = (pl.program_id(0) % tiles_per_seq) == 0
    parts = ((q_ref, qp_ref, qo_ref, True), (k_ref, kp_ref, ko_ref, True), (v_ref, vp_ref, vo_ref, False))
    for n, (cur, prev, out, normed) in enumerate(parts):
        ext_ref[0:HALO, :] = jnp.where(first, 0.0, prev[...])
        ext_ref[HALO:HALO + tl, :] = cur[...]
        cw = cw_ref[:, n * w:(n + 1) * w]
        acc = cw[B_CONV - 1:B_CONV, :] * cur[...]
        for j in range(B_CONV - 1):
            off = HALO - (B_CONV - 1) + j
            acc = acc + cw[j:j + 1, :] * ext_ref[off:off + tl, :]
        y = acc * jax.nn.sigmoid(acc)
        if normed:
            for h in range(B_HEADS):
                cols = slice(h * HEAD_DIM, (h + 1) * HEAD_DIM)
                yh = y[:, cols]
                out[:, cols] = yh * lax.rsqrt(jnp.sum(yh * yh, axis=-1, keepdims=True) + EPS)
        else:
            out[...] = y
    a = sm_ref[:, SM_BA:SM_BA + B_HEADS] + dtb_ref[...]
    softplus = jnp.maximum(a, 0.0) + jnp.log(1.0 + jnp.exp(-jnp.abs(a)))
    g_ref[...] = -jnp.exp(alog_ref[...]) * softplus
    beta_ref[...] = jax.nn.sigmoid(sm_ref[:, SM_BB:SM_BB + B_HEADS])


def deltanet_prep(proj, conv_w, a_log, dt_bias, seq_len, tl=256):
    t = proj.shape[0]
    w = BRANCH_W
    cb = P_BQ // w
    cur = lambda c: pl.BlockSpec((tl, w), lambda i: (i, c))
    prev = lambda c: pl.BlockSpec((HALO, w), lambda i: (jnp.maximum(i * (tl // HALO) - 1, 0), c))
    row = lambda width: pl.BlockSpec((tl, width), lambda i: (i, 0))
    return pl.pallas_call(
        functools.partial(_b_prep_kernel, tiles_per_seq=seq_len // tl),
        grid=(t // tl,),
        in_specs=[cur(cb), cur(cb + 1), cur(cb + 2), prev(cb), prev(cb + 1), prev(cb + 2),
                  pl.BlockSpec((tl, HEAD_DIM), lambda i: (i, P_SM // HEAD_DIM)),
                  pl.BlockSpec((B_CONV, 3 * w), lambda i: (0, 0)),
                  pl.BlockSpec((1, B_HEADS), lambda i: (0, 0)),
                  pl.BlockSpec((1, B_HEADS), lambda i: (0, 0))],
        out_specs=[row(w), row(w), row(w), row(B_HEADS), row(B_HEADS)],
        out_shape=[jax.ShapeDtypeStruct((t, w), jnp.float32)] * 3
        + [jax.ShapeDtypeStruct((t, B_HEADS), jnp.float32)] * 2,
        scratch_shapes=[pltpu.VMEM((tl + HALO, w), jnp.float32)],
        compiler_params=_cparams("parallel"),
        name="mixer_b_prep",
    )(proj, proj, proj, proj, proj, proj, proj, conv_w, a_log.reshape(1, B_HEADS), dt_bias.reshape(1, B_HEADS))


def _delta_kernel(q_ref, k_ref, v_ref, g_ref, beta_ref, z_ref, gain_ref, o_ref, state_ref):
    cs = B_CHUNK
    hi = lax.Precision.HIGHEST
    bf = MXU_DTYPE

    @pl.when(pl.program_id(1) == 0)
    def _():
        state_ref[...] = jnp.zeros(state_ref.shape, jnp.float32)

    row = lax.broadcasted_iota(jnp.int32, (cs, cs), 0)
    col = lax.broadcasted_iota(jnp.int32, (cs, cs), 1)
    incl = row >= col
    strict = row > col
    eye_h = (lax.broadcasted_iota(jnp.int32, (B_HEADS, B_HEADS), 0)
             == lax.broadcasted_iota(jnp.int32, (B_HEADS, B_HEADS), 1)).astype(jnp.float32)
    eye_c = (row == col).astype(jnp.float32)
    eye_d = (lax.broadcasted_iota(jnp.int32, (HEAD_DIM, HEAD_DIM), 0)
             == lax.broadcasted_iota(jnp.int32, (HEAD_DIM, HEAD_DIM), 1)).astype(bf)

    gc = _dot(incl.astype(jnp.float32), g_ref[...], precision=hi)
    gc_t = _nt_dot(eye_h, gc, precision=hi)
    eg = jnp.exp(gc)
    g_last = gc[cs - 1:cs, :]
    ek = jnp.exp(g_last - gc)
    beta = beta_ref[...]
    gain = gain_ref[...]
    qscale = HEAD_DIM ** -0.5

    for h in range(B_HEADS):
        cols = slice(h * HEAD_DIM, (h + 1) * HEAD_DIM)
        qh = q_ref[:, cols] * qscale
        kh = k_ref[:, cols]
        bh = beta[:, h:h + 1]
        kb = kh * bh
        vb = v_ref[:, cols] * bh
        diff = gc[:, h:h + 1] - gc_t[h:h + 1, :]
        decay = jnp.exp(jnp.where(incl, diff, -jnp.inf))
        kh_b = kh.astype(bf)
        lower = jnp.where(strict, _nt_dot(kb.astype(bf), kh_b) * decay, 0.0)
        inv = eye_c - lower
        pw = lower
        for _ in range(5):
            pw = _dot(pw, pw, precision=hi)
            inv = inv + _dot(inv, pw, precision=hi)
        inv_b = inv.astype(bf)
        u = _dot(inv_b, vb.astype(bf))
        wmat = _dot(inv_b, (kb * eg[:, h:h + 1]).astype(bf))
        attn = jnp.where(incl, _nt_dot(qh.astype(bf), kh_b) * decay, 0.0)
        q_dec = qh * eg[:, h:h + 1]
        k_dec = kh * ek[:, h:h + 1]
        state = state_ref[h]
        state_b = state.astype(bf)
        v_new = u - _dot(wmat.astype(bf), state_b)
        v_new_b = v_new.astype(bf)
        out = _dot(q_dec.astype(bf), state_b) + _dot(attn.astype(bf), v_new_b)
        k_dec_t = _nt_dot(eye_d, k_dec.astype(bf)).astype(bf)
        state_ref[h] = state * jnp.exp(g_last[:, h:h + 1]) + _dot(k_dec_t, v_new_b)
        on = out * lax.rsqrt(jnp.mean(out * out, axis=-1, keepdims=True) + EPS) * gain
        z = z_ref[:, cols]
        o_ref[:, cols] = (on * (z * jax.nn.sigmoid(z))).astype(o_ref.dtype)


def deltanet_mixer(proj, q, k, v, g, beta, out_gain, batch, seq_len):
    t = proj.shape[0]
    w = BRANCH_W
    cs = B_CHUNK
    nc = seq_len // cs
    blk = lambda width: pl.BlockSpec((cs, width), lambda b, c: (b * nc + c, 0))
    return pl.pallas_call(
        _delta_kernel,
        grid=(batch, nc),
        in_specs=[blk(w), blk(w), blk(w), blk(B_HEADS), blk(B_HEADS),
                  pl.BlockSpec((cs, w), lambda b, c: (b * nc + c, P_BZ // w)),
                  pl.BlockSpec((1, HEAD_DIM), lambda b, c: (0, 0))],
        out_specs=blk(w),
        out_shape=jax.ShapeDtypeStruct((t, w), MXU_DTYPE),
        scratch_shapes=[pltpu.VMEM((B_HEADS, HEAD_DIM, HEAD_DIM), jnp.float32)],
        compiler_params=_cparams("parallel", "arbitrary"),
        name="mixer_b",
    )(q, k, v, g, beta, proj, out_gain.reshape(1, HEAD_DIM))


def _merge_kernel(oa_ref, ob_ref, oc_ref, od_ref, g_ref, wb_ref, wg_ref, o_ref):
    lat = g_ref[...].astype(MXU_DTYPE)
    acc = None
    for n, o in enumerate((oa_ref, ob_ref, oc_ref, od_ref)):
        y = _dot(o[...], wb_ref[n])
        gate = jax.nn.sigmoid(_dot(lat, wg_ref[n]))
        acc = gate * y if acc is None else acc + gate * y
    o_ref[...] = acc.astype(o_ref.dtype)


def merge_branches(outs, proj, w_branch, w_gate, tm=1024, tn=512):
    t = proj.shape[0]
    w = BRANCH_W
    tm = min(tm, t)
    n_br = len(outs)
    return pl.pallas_call(
        _merge_kernel,
        grid=(t // tm, D_MODEL // tn),
        in_specs=[pl.BlockSpec((tm, w), lambda i, j: (i, 0))] * n_br
        + [pl.BlockSpec((tm, GATE_RANK), lambda i, j: (i, P_G // GATE_RANK)),
           pl.BlockSpec((n_br, w, tn), lambda i, j: (0, 0, j)),
           pl.BlockSpec((n_br, GATE_RANK, tn), lambda i, j: (0, 0, j))],
        out_specs=pl.BlockSpec((tm, tn), lambda i, j: (i, j)),
        out_shape=jax.ShapeDtypeStruct((t, D_MODEL), MXU_DTYPE),
        compiler_params=_cparams("parallel", "parallel"),
        name="merge",
    )(*outs, proj, w_branch, w_gate)


def _permute_w_in(w):
    z = jnp.zeros(w.shape[:-1] + (N_P - 11872,), w.dtype)
    return jnp.concatenate([
        w[..., 0:1024],
        w[..., 1280:2304],
        w[..., 2384:5456],
        w[..., 5456:6480],
        w[..., 6496:8544],
        w[..., 8544:11616],
        w[..., 11616:11872],
        w[..., 1024:1152],
        w[..., 1152:1280],
        w[..., 2304:2368],
        w[..., 2368:2384],
        w[..., 6480:6488],
        w[..., 6488:6496],
        z], axis=-1)


def kernel(x, mix_norm, w_in, a_q_norm, a_k_norm, a_idx_k_norm, b_conv, b_a_log, b_dt_bias, b_out_norm,
           c_ln_gain, c_ln_bias, c_spatial_w, c_spatial_b, d_conv, w_branch, w_gate_up, w_out, ffn_norm,
           w_ff1, w_ff2):
    bn, seq_len, d = x.shape
    depth = w_in.shape[0]
    bf = MXU_DTYPE
    w_in_p = _permute_w_in(w_in).astype(bf)
    w_branch_b = w_branch.astype(bf).reshape(depth, 4, BRANCH_W, d)
    w_gate_b = w_gate_up.astype(bf)
    w_out_b = w_out.astype(bf)
    w_ff1_b = w_ff1.astype(bf)
    w_ff2_b = w_ff2.astype(bf)

    xt = x.reshape(bn * seq_len, d)
    for l in range(depth):
        h = rmsnorm_rows(xt, mix_norm[l])
        proj = matmul(h, w_in_p[l])
        kn, vb, ki = dsa_prep(proj, a_k_norm[l], a_idx_k_norm[l])
        o_a = dsa_mixer(proj, kn, vb, ki, a_q_norm[l], bn, seq_len)
        bq, bk, bv, bg, bbeta = deltanet_prep(proj, b_conv[l], b_a_log[l], b_dt_bias[l], seq_len)
        o_b = deltanet_mixer(proj, bq, bk, bv, bg, bbeta, b_out_norm[l], bn, seq_len)
        o_c = spatial_gating_mixer(proj, c_ln_gain[l], c_ln_bias[l], c_spatial_w[l], c_spatial_b[l])
        o_d = short_conv_mixer(proj, d_conv[l], seq_len)
        merged = merge_branches((o_a, o_b, o_c, o_d), proj, w_branch_b[l], w_gate_b[l])
        xt = matmul(merged, w_out_b[l], epilogue="residual", residual=xt, tn=512)
        f = rmsnorm_rows(xt, ffn_norm[l])
        mid = matmul(f, w_ff1_b[l], epilogue="relu2", out_dtype=bf)
        xt = matmul(mid, w_ff2_b[l], epilogue="residual", residual=xt, tn=512)
    return xt.reshape(bn, seq_len, d)
```

```python
import functools

import jax
import jax.numpy as jnp
from jax import lax
from jax.experimental import pallas as pl
from jax.experimental.pallas import tpu as pltpu

D_MODEL = 4096
HEAD_DIM = 128
BRANCH_W = 1024
A_HEADS = 8
IDX_HEADS = 16
IDX_DIM = 64
IDX_W_SCALE = (IDX_HEADS * IDX_DIM) ** -0.5
TOPK_MAX = 256
Q_BLOCK = 128
B_HEADS = 8
B_CONV = 4
B_CHUNK = 64
C_CHUNK = 128
C_GROUPS = 8
D_CONV = 3
GATE_RANK = 256
EPS = 1e-6

P_AQ, P_IQ, P_BQ, P_BZ, P_CU, P_CV, P_DH, P_DB, P_DC = 0, 1024, 2048, 5120, 6144, 7168, 8192, 9216, 10240
P_G, P_AK, P_AV, P_SM = 11264, 11520, 11648, 11776
N_P = 11904
SM_IK, SM_IW, SM_BA, SM_BB = 0, 64, 80, 88

MXU_DTYPE = jnp.bfloat16
HALO = 8
VMEM_LIMIT = 56 * 1024 * 1024
NEG_BIG = -1e30
INT_MIN = -(2 ** 31)
KEY_NEG_INF = INT_MIN + 0x7FFFFF
HIGHEST = lax.Precision.HIGHEST


def _cparams(*sem):
    return pltpu.CompilerParams(dimension_semantics=sem, vmem_limit_bytes=VMEM_LIMIT)


def _nt_dot(a, b, precision=None):
    return lax.dot_general(a, b, (((1,), (1,)), ((), ())), precision=precision,
                           preferred_element_type=jnp.float32)


def _dot(a, b, precision=None):
    return jnp.dot(a, b, precision=precision, preferred_element_type=jnp.float32)


def _dot3(a, b):
    a_hi, b_hi = a.astype(MXU_DTYPE), b.astype(MXU_DTYPE)
    a_lo = (a - a_hi.astype(jnp.float32)).astype(MXU_DTYPE)
    b_lo = (b - b_hi.astype(jnp.float32)).astype(MXU_DTYPE)
    return _dot(a_hi, b_hi) + (_dot(a_hi, b_lo) + _dot(a_lo, b_hi))


def _eye(n, dtype):
    r = lax.broadcasted_iota(jnp.int32, (n, n), 0)
    c = lax.broadcasted_iota(jnp.int32, (n, n), 1)
    return (r == c).astype(dtype)


def _rmsnorm_kernel(x_ref, g_ref, o_ref):
    x = x_ref[...]
    ms = jnp.mean(x * x, axis=-1, keepdims=True)
    o_ref[...] = (x * lax.rsqrt(ms + EPS) * g_ref[...]).astype(o_ref.dtype)


def rmsnorm_rows(x, gain, tl=256):
    t, d = x.shape
    return pl.pallas_call(
        _rmsnorm_kernel,
        grid=(t // tl,),
        in_specs=[pl.BlockSpec((tl, d), lambda i: (i, 0)),
                  pl.BlockSpec((1, d), lambda i: (0, 0))],
        out_specs=pl.BlockSpec((tl, d), lambda i: (i, 0)),
        out_shape=jax.ShapeDtypeStruct((t, d), MXU_DTYPE),
        compiler_params=_cparams("parallel"),
        name="rmsnorm",
    )(x, gain.reshape(1, d))


def _mm_kernel(a_ref, w_ref, *rest, nk, epilogue):
    if epilogue == "residual":
        r_ref, o_ref = rest[0], rest[1]
        scratch = rest[2:]
    else:
        r_ref, o_ref = None, rest[0]
        scratch = rest[1:]
    part = _dot(a_ref[...], w_ref[...])

    def finish(acc):
        if epilogue == "residual":
            o_ref[...] = r_ref[...] + acc
        elif epilogue == "relu2":
            o_ref[...] = jnp.square(jnp.maximum(acc, 0.0)).astype(o_ref.dtype)
        else:
            o_ref[...] = acc.astype(o_ref.dtype)

    if nk == 1:
        finish(part)
    else:
        acc_ref = scratch[0]
        k = pl.program_id(2)

        @pl.when(k == 0)
        def _():
            acc_ref[...] = part

        @pl.when(k > 0)
        def _():
            acc_ref[...] += part

        @pl.when(k == nk - 1)
        def _():
            finish(acc_ref[...])


def matmul(a, w_stack, layer, *, epilogue="none", residual=None, out_dtype=jnp.float32, tm=1024, tn=1024, tk=4096):
    m, kdim = a.shape
    n = w_stack.shape[2]
    tm, tn, tk = min(tm, m), min(tn, n), min(tk, kdim)
    nk = kdim // tk
    in_specs = [pl.BlockSpec((tm, tk), lambda i, j, k: (i, k)),
                pl.BlockSpec((None, tk, tn), lambda i, j, k: (layer, k, j))]
    args = [a, w_stack]
    if epilogue == "residual":
        in_specs.append(pl.BlockSpec((tm, tn), lambda i, j, k: (i, j)))
        args.append(residual)
    scratch = [pltpu.VMEM((tm, tn), jnp.float32)] if nk > 1 else []
    return pl.pallas_call(
        functools.partial(_mm_kernel, nk=nk, epilogue=epilogue),
        grid=(m // tm, pl.cdiv(n, tn), nk),
        in_specs=in_specs,
        out_specs=pl.BlockSpec((tm, tn), lambda i, j, k: (i, j)),
        out_shape=jax.ShapeDtypeStruct((m, n), out_dtype),
        scratch_shapes=scratch,
        compiler_params=_cparams("parallel", "parallel", "arbitrary"),
        name="matmul_" + epilogue,
    )(*args)


def _dconv_kernel(h_ref, b_ref, c_ref, hp_ref, cp_ref, w_ref, o_ref, ext_ref, *, tiles_per_seq):
    tl = h_ref.shape[0]
    first = (pl.program_id(0) % tiles_per_seq) == 0
    ext_ref[0:HALO, :] = jnp.where(first, 0.0, cp_ref[...] * hp_ref[...])
    p = c_ref[...] * h_ref[...]
    ext_ref[HALO:HALO + tl, :] = p
    w = w_ref[...]
    acc = w[2:3, :] * p
    acc = acc + w[1:2, :] * ext_ref[HALO - 1:HALO - 1 + tl, :]
    acc = acc + w[0:1, :] * ext_ref[HALO - 2:HALO - 2 + tl, :]
    o_ref[...] = (b_ref[...] * acc).astype(o_ref.dtype)


def short_conv_mixer(proj, conv_w, seq_len, tl=512):
    t = proj.shape[0]
    w = BRANCH_W
    cur = lambda cb: pl.BlockSpec((tl, w), lambda i: (i, cb))
    prev = lambda cb: pl.BlockSpec((HALO, w), lambda i: (jnp.maximum(i * (tl // HALO) - 1, 0), cb))
    return pl.pallas_call(
        functools.partial(_dconv_kernel, tiles_per_seq=seq_len // tl),
        grid=(t // tl,),
        in_specs=[cur(P_DH // w), cur(P_DB // w), cur(P_DC // w), prev(P_DH // w), prev(P_DC // w),
                  pl.BlockSpec((D_CONV, w), lambda i: (0, 0))],
        out_specs=pl.BlockSpec((tl, w), lambda i: (i, 0)),
        out_shape=jax.ShapeDtypeStruct((t, w), MXU_DTYPE),
        scratch_shapes=[pltpu.VMEM((tl + HALO, w), jnp.float32)],
        compiler_params=_cparams("parallel"),
        name="mixer_d",
    )(proj, proj, proj, proj, proj, conv_w)


def _gmlp_kernel(u_ref, v_ref, lg_ref, lb_ref, ws_ref, bs_ref, o_ref):
    tl = u_ref.shape[0]
    gd = BRANCH_W // C_GROUPS
    row = lax.broadcasted_iota(jnp.int32, (C_CHUNK, C_CHUNK), 0)
    col = lax.broadcasted_iota(jnp.int32, (C_CHUNK, C_CHUNK), 1)
    causal = row >= col
    for g in range(C_GROUPS):
        w_m = jnp.where(causal, ws_ref[g], 0.0).astype(MXU_DTYPE)
        bias = bs_ref[:, g:g + 1]
        gain = lg_ref[:, g * gd:(g + 1) * gd]
        shift = lb_ref[:, g * gd:(g + 1) * gd]
        for c in range(tl // C_CHUNK):
            rows = slice(c * C_CHUNK, (c + 1) * C_CHUNK)
            cols = slice(g * gd, (g + 1) * gd)
            v = jax.nn.gelu(v_ref[rows, cols])
            mu = jnp.mean(v, axis=-1, keepdims=True)
            vc = v - mu
            var = jnp.mean(vc * vc, axis=-1, keepdims=True)
            vn = vc * lax.rsqrt(var + EPS) * gain + shift
            s = _dot(w_m, vn.astype(MXU_DTYPE)) + bias
            o_ref[rows, cols] = (jax.nn.gelu(u_ref[rows, cols]) * s).astype(o_ref.dtype)


def spatial_gating_mixer(proj, ln_gain, ln_bias, w_s, b_s, tl=256):
    t = proj.shape[0]
    w = BRANCH_W
    return pl.pallas_call(
        _gmlp_kernel,
        grid=(t // tl,),
        in_specs=[pl.BlockSpec((tl, w), lambda i: (i, P_CU // w)),
                  pl.BlockSpec((tl, w), lambda i: (i, P_CV // w)),
                  pl.BlockSpec((1, w), lambda i: (0, 0)),
                  pl.BlockSpec((1, w), lambda i: (0, 0)),
                  pl.BlockSpec((C_GROUPS, C_CHUNK, C_CHUNK), lambda i: (0, 0, 0)),
                  pl.BlockSpec((C_CHUNK, C_GROUPS), lambda i: (0, 0))],
        out_specs=pl.BlockSpec((tl, w), lambda i: (i, 0)),
        out_shape=jax.ShapeDtypeStruct((t, w), MXU_DTYPE),
        compiler_params=_cparams("parallel"),
        name="mixer_c",
    )(proj, proj, ln_gain.reshape(1, w), ln_bias.reshape(1, w), w_s, b_s.T)


def _a_prep_kernel(k_ref, v_ref, sm_ref, gk_ref, gi_ref, kn_ref, vt_ref, ki_ref):
    k = k_ref[...]
    kn = k * lax.rsqrt(jnp.mean(k * k, axis=-1, keepdims=True) + EPS) * gk_ref[...]
    kn_ref[...] = kn.astype(kn_ref.dtype)
    vt_ref[...] = _nt_dot(_eye(HEAD_DIM, MXU_DTYPE), v_ref[...].astype(MXU_DTYPE)).astype(vt_ref.dtype)
    ik = sm_ref[:, SM_IK:SM_IK + IDX_DIM]
    ikn = ik * lax.rsqrt(jnp.mean(ik * ik, axis=-1, keepdims=True) + EPS) * gi_ref[...]
    ki_ref[...] = ikn.astype(ki_ref.dtype)


def dsa_prep(proj, k_gain, ik_gain, tl=512):
    t = proj.shape[0]
    hd = HEAD_DIM
    return pl.pallas_call(
        _a_prep_kernel,
        grid=(t // tl,),
        in_specs=[pl.BlockSpec((tl, hd), lambda i: (i, P_AK // hd)),
                  pl.BlockSpec((tl, hd), lambda i: (i, P_AV // hd)),
                  pl.BlockSpec((tl, hd), lambda i: (i, P_SM // hd)),
                  pl.BlockSpec((1, hd), lambda i: (0, 0)),
                  pl.BlockSpec((1, IDX_DIM), lambda i: (0, 0))],
        out_specs=[pl.BlockSpec((tl, hd), lambda i: (i, 0)),
                   pl.BlockSpec((hd, tl), lambda i: (0, i)),
                   pl.BlockSpec((tl, IDX_DIM), lambda i: (i, 0))],
        out_shape=[jax.ShapeDtypeStruct((t, hd), MXU_DTYPE),
                   jax.ShapeDtypeStruct((hd, t), MXU_DTYPE),
                   jax.ShapeDtypeStruct((t, IDX_DIM), MXU_DTYPE)],
        compiler_params=_cparams("parallel"),
        name="mixer_a_prep",
    )(proj, proj, proj, k_gain.reshape(1, hd), ik_gain.reshape(1, IDX_DIM))


def _dsa_kernel(q_ref, iq_ref, sm_ref, kn_ref, vt_ref, ki_ref, gq_ref, o_ref,
                keys_ref, qt_ref, qit_ref, m_ref, l_ref, acc_ref, *, topk, ck, seq_len):
    qb = Q_BLOCK
    bf = MXU_DTYPE
    f32 = jnp.float32
    i = pl.program_id(1)
    n_chunks = (i * qb + qb + ck - 1) // ck

    def chunk_start(c):
        return pl.multiple_of(c * ck, ck)

    def col_sum(x):
        return jnp.sum(jnp.sum(x.reshape(4, ck // 32, 8, qb), axis=1), axis=0)

    gq = gq_ref[...]
    eye_d = _eye(HEAD_DIM, bf)
    for h in range(A_HEADS):
        qh = q_ref[:, h * HEAD_DIM:(h + 1) * HEAD_DIM]
        qn = qh * lax.rsqrt(jnp.mean(qh * qh, axis=-1, keepdims=True) + EPS) * gq
        qt_ref[:, h * qb:(h + 1) * qb] = _nt_dot(eye_d, qn.astype(bf)).astype(bf)
    eye_i = _eye(IDX_DIM, bf)
    for h in range(IDX_HEADS):
        qi = iq_ref[:, h * IDX_DIM:(h + 1) * IDX_DIM].astype(bf)
        qit_ref[:, h * qb:(h + 1) * qb] = _nt_dot(eye_i, qi).astype(bf)
    w_t = _nt_dot(_eye(IDX_HEADS, f32), sm_ref[:, SM_IW:SM_IW + IDX_HEADS] * IDX_W_SCALE,
                  precision=HIGHEST)

    q_pos = i * qb + lax.broadcasted_iota(jnp.int32, (ck, qb), 1)
    k_off = lax.broadcasted_iota(jnp.int32, (ck, qb), 0)

    def score_chunk(c, carry):
        start = chunk_start(c)
        kc = ki_ref[pl.ds(start, ck), :]
        acc = jnp.zeros((ck, qb), f32)
        for hp in range(IDX_HEADS // 2):
            r = _dot(kc, qit_ref[:, hp * 2 * qb:(hp + 1) * 2 * qb])
            for s in range(2):
                h = 2 * hp + s
                acc = acc + w_t[h:h + 1, :] * jnp.maximum(r[:, s * qb:(s + 1) * qb], 0.0)
        score = jnp.where(k_off + start <= q_pos, acc, -jnp.inf)
        bits = pltpu.bitcast(score, jnp.int32)
        keys_ref[pl.ds(start, ck), :] = bits ^ ((bits >> 31) & 0x7FFFFFFF)
        return carry

    lax.fori_loop(0, n_chunks, score_chunk, 0)

    def count(pred):
        def body(c, cnt):
            start = chunk_start(c)
            hit = pred(keys_ref[pl.ds(start, ck), :], k_off + start)
            return cnt + col_sum(jnp.where(hit, 1.0, 0.0))
        cnt = lax.fori_loop(0, n_chunks, body, jnp.zeros((8, qb), f32))
        return jnp.sum(cnt, axis=0, keepdims=True)

    def bit_step(b, tau_u):
        cand_u = tau_u | lax.shift_left(jnp.int32(1), 31 - b)
        cand_s = cand_u ^ INT_MIN
        total = count(lambda kk, _: kk >= cand_s)
        return jnp.where(total >= float(topk), cand_u, tau_u)

    tau_u = lax.fori_loop(0, 32, bit_step, jnp.zeros((1, qb), jnp.int32))
    tau = jnp.maximum(tau_u ^ INT_MIN, KEY_NEG_INF + 1)

    need = float(topk) - count(lambda kk, _: kk > tau)
    n_equal = count(lambda kk, _: kk == tau)
    has_tie = jnp.max(jnp.where(n_equal > need, 1.0, 0.0)) > 0.5

    @pl.when(has_tie)
    def _():
        nbits = seq_len.bit_length()

        def tie_bit(b, x):
            cand = x | lax.shift_left(jnp.int32(1), nbits - 1 - b)
            below = count(lambda kk, kidx: jnp.where(kk == tau, kidx, cand) < cand)
            return jnp.where(below < need, cand, x)

        x = lax.fori_loop(0, nbits, tie_bit, jnp.zeros((1, qb), jnp.int32))

        def demote(c, carry):
            start = chunk_start(c)
            kk = keys_ref[pl.ds(start, ck), :]
            drop = jnp.where(kk == tau, k_off + start, x) > x
            keys_ref[pl.ds(start, ck), :] = jnp.where(drop, KEY_NEG_INF, kk)
            return carry

        lax.fori_loop(0, n_chunks, demote, 0)

    m_ref[...] = jnp.full(m_ref.shape, NEG_BIG, f32)
    l_ref[...] = jnp.zeros(l_ref.shape, f32)
    acc_ref[...] = jnp.zeros(acc_ref.shape, f32)
    c_exp = (HEAD_DIM ** -0.5) * 1.4426950408889634
    pairs = [slice(hp * 2 * qb, (hp + 1) * 2 * qb) for hp in range(A_HEADS // 2)]

    def attn_chunk(c, carry):
        start = chunk_start(c)
        kc = kn_ref[pl.ds(start, ck), :]
        vt = vt_ref[:, pl.ds(start, ck)]
        sel = keys_ref[pl.ds(start, ck), :] >= tau
        raw = [_dot(kc, qt_ref[:, cols]) for cols in pairs]
        probs, alphas = [], []
        for cols, s in zip(pairs, raw):
            s = jnp.concatenate([jnp.where(sel, s[:, :qb], NEG_BIG), jnp.where(sel, s[:, qb:], NEG_BIG)], axis=1)
            m_old = m_ref[:, cols]
            m_new = jnp.maximum(m_old, jnp.max(s, axis=0, keepdims=True))
            alpha = jnp.exp2((m_old - m_new) * c_exp)
            p = jnp.exp2((s - m_new) * c_exp)
            l_ref[:, cols] = alpha * l_ref[:, cols] + jnp.sum(p, axis=0, keepdims=True)
            m_ref[:, cols] = m_new
            probs.append(p.astype(bf))
            alphas.append(alpha)
        for cols, p, alpha in zip(pairs, probs, alphas):
            acc_ref[:, cols] = alpha * acc_ref[:, cols] + _dot(vt, p)
        return carry

    lax.fori_loop(0, n_chunks, attn_chunk, 0)

    for h in range(A_HEADS):
        cols = slice(h * qb, (h + 1) * qb)
        o_t = acc_ref[:, cols] / l_ref[:, cols]
        o_ref[:, h * HEAD_DIM:(h + 1) * HEAD_DIM] = o_t.T.astype(o_ref.dtype)


def dsa_mixer(proj, kn, vt, ki, q_gain, batch, seq_len, ck=512):
    t = proj.shape[0]
    qb = Q_BLOCK
    nq = seq_len // qb
    ck = min(ck, seq_len)
    topk = min(TOPK_MAX, seq_len // 4)
    w = BRANCH_W
    return pl.pallas_call(
        functools.partial(_dsa_kernel, topk=topk, ck=ck, seq_len=seq_len),
        grid=(batch, nq),
        in_specs=[pl.BlockSpec((qb, w), lambda b, i: (b * nq + i, P_AQ // w)),
                  pl.BlockSpec((qb, w), lambda b, i: (b * nq + i, P_IQ // w)),
                  pl.BlockSpec((qb, HEAD_DIM), lambda b, i: (b * nq + i, P_SM // HEAD_DIM)),
                  pl.BlockSpec((seq_len, HEAD_DIM), lambda b, i: (b, 0)),
                  pl.BlockSpec((HEAD_DIM, seq_len), lambda b, i: (0, b)),
                  pl.BlockSpec((seq_len, IDX_DIM), lambda b, i: (b, 0)),
                  pl.BlockSpec((1, HEAD_DIM), lambda b, i: (0, 0))],
        out_specs=pl.BlockSpec((qb, w), lambda b, i: (b * nq + i, 0)),
        out_shape=jax.ShapeDtypeStruct((t, w), MXU_DTYPE),
        scratch_shapes=[pltpu.VMEM((seq_len, qb), jnp.int32),
                        pltpu.VMEM((HEAD_DIM, A_HEADS * qb), MXU_DTYPE),
                        pltpu.VMEM((IDX_DIM, IDX_HEADS * qb), MXU_DTYPE),
                        pltpu.VMEM((1, A_HEADS * qb), jnp.float32),
                        pltpu.VMEM((1, A_HEADS * qb), jnp.float32),
                        pltpu.VMEM((HEAD_DIM, A_HEADS * qb), jnp.float32)],
        compiler_params=_cparams("parallel", "arbitrary"),
        name="mixer_a",
    )(proj, proj, proj, kn, vt, ki, q_gain.reshape(1, HEAD_DIM))


def _b_prep_kernel(q_ref, k_ref, v_ref, qp_ref, kp_ref, vp_ref, sm_ref, cw_ref, alog_ref, dtb_ref,
                   qo_ref, ko_ref, vo_ref, g_ref, beta_ref, ext_ref, *, tiles_per_seq):
    tl = q_ref.shape[0]
    w = BRANCH_W
    first = (pl.program_id(0) % tiles_per_seq) == 0
    parts = ((q_ref, qp_ref, qo_ref, True), (k_ref, kp_ref, ko_ref, True), (v_ref, vp_ref, vo_ref, False))
    for n, (cur, prev, out, normed) in enumerate(parts):
        ext_ref[0:HALO, :] = jnp.where(first, 0.0, prev[...])
        ext_ref[HALO:HALO + tl, :] = cur[...]
        cw = cw_ref[:, n * w:(n + 1) * w]
        acc = cw[B_CONV - 1:B_CONV, :] * cur[...]
        for j in range(B_CONV - 1):
            off = HALO - (B_CONV - 1) + j
            acc = acc + cw[j:j + 1, :] * ext_ref[off:off + tl, :]
        y = acc * jax.nn.sigmoid(acc)
        if normed:
            for h in range(B_HEADS):
                cols = slice(h * HEAD_DIM, (h + 1) * HEAD_DIM)
                yh = y[:, cols]
                out[:, cols] = yh * lax.rsqrt(jnp.sum(yh * yh, axis=-1, keepdims=True) + EPS)
        else:
            out[...] = y
    a = sm_ref[:, SM_BA:SM_BA + B_HEADS] + dtb_ref[...]
    softplus = jnp.maximum(a, 0.0) + jnp.log(1.0 + jnp.exp(-jnp.abs(a)))
    g_ref[...] = -jnp.exp(alog_ref[...]) * softplus
    beta_ref[...] = jax.nn.sigmoid(sm_ref[:, SM_BB:SM_BB + B_HEADS])


def deltanet_prep(proj, conv_w, a_log, dt_bias, seq_len, tl=256):
    t = proj.shape[0]
    w = BRANCH_W
    cb = P_BQ // w
    cur = lambda c: pl.BlockSpec((tl, w), lambda i: (i, c))
    prev = lambda c: pl.BlockSpec((HALO, w), lambda i: (jnp.maximum(i * (tl // HALO) - 1, 0), c))
    row = lambda width: pl.BlockSpec((tl, width), lambda i: (i, 0))
    return pl.pallas_call(
        functools.partial(_b_prep_kernel, tiles_per_seq=seq_len // tl),
        grid=(t // tl,),
        in_specs=[cur(cb), cur(cb + 1), cur(cb + 2), prev(cb), prev(cb + 1), prev(cb + 2),
                  pl.BlockSpec((tl, HEAD_DIM), lambda i: (i, P_SM // HEAD_DIM)),
                  pl.BlockSpec((B_CONV, 3 * w), lambda i: (0, 0)),
                  pl.BlockSpec((1, B_HEADS), lambda i: (0, 0)),
                  pl.BlockSpec((1, B_HEADS), lambda i: (0, 0))],
        out_specs=[row(w), row(w), row(w), row(B_HEADS), row(B_HEADS)],
        out_shape=[jax.ShapeDtypeStruct((t, w), jnp.float32)] * 3
        + [jax.ShapeDtypeStruct((t, B_HEADS), jnp.float32)] * 2,
        scratch_shapes=[pltpu.VMEM((tl + HALO, w), jnp.float32)],
        compiler_params=_cparams("parallel"),
        name="mixer_b_prep",
    )(proj, proj, proj, proj, proj, proj, proj, conv_w, a_log.reshape(1, B_HEADS), dt_bias.reshape(1, B_HEADS))


def _delta_kernel(q_ref, k_ref, v_ref, g_ref, beta_ref, z_ref, gain_ref, o_ref, state_ref):
    nb = q_ref.shape[0]
    cs = B_CHUNK
    bf = MXU_DTYPE
    f32 = jnp.float32

    @pl.when(pl.program_id(0) == 0)
    def _():
        state_ref[...] = jnp.zeros(state_ref.shape, f32)

    row = lax.broadcasted_iota(jnp.int32, (cs, cs), 0)
    col = lax.broadcasted_iota(jnp.int32, (cs, cs), 1)
    incl = row >= col
    strict = row > col
    eye_c = (row == col).astype(f32)
    eye_h = _eye(B_HEADS, f32)
    eye_d = _eye(HEAD_DIM, bf)
    tril = incl.astype(f32)
    gain = gain_ref[...]
    qscale = HEAD_DIM ** -0.5

    chains = []
    for b in range(nb):
        gc = _dot(tril, g_ref[b], precision=HIGHEST)
        gc_t = _nt_dot(eye_h, gc, precision=HIGHEST)
        eg = jnp.exp(gc)
        g_last = gc[cs - 1:cs, :]
        ek = jnp.exp(g_last - gc)
        e_last = jnp.exp(g_last)
        beta = beta_ref[b]
        for h in range(B_HEADS):
            cols = slice(h * HEAD_DIM, (h + 1) * HEAD_DIM)
            qh = q_ref[b, :, cols] * qscale
            kh = k_ref[b, :, cols]
            bh = beta[:, h:h + 1]
            kb = kh * bh
            decay = jnp.exp(jnp.where(incl, gc[:, h:h + 1] - gc_t[h:h + 1, :], -jnp.inf))
            chains.append(dict(
                b=b, h=h, cols=cols, idx=b * B_HEADS + h, decay=decay,
                q_b=qh.astype(bf), k_b=kh.astype(bf), kb_b=kb.astype(bf),
                vb_b=(v_ref[b, :, cols] * bh).astype(bf),
                kbe_b=(kb * eg[:, h:h + 1]).astype(bf),
                qdec_b=(qh * eg[:, h:h + 1]).astype(bf),
                kdec_b=(kh * ek[:, h:h + 1]).astype(bf),
                e_last=e_last[:, h:h + 1]))

    for ch in chains:
        ch["lower"] = jnp.where(strict, _nt_dot(ch["kb_b"], ch["k_b"]) * ch["decay"], 0.0)
    for ch in chains:
        ch["attn_b"] = jnp.where(incl, _nt_dot(ch["q_b"], ch["k_b"]) * ch["decay"], 0.0).astype(bf)
    for ch in chains:
        ch["inv"] = eye_c - ch["lower"]
        ch["pw"] = ch["lower"]
    for _ in range(5):
        for ch in chains:
            ch["pw"] = _dot3(ch["pw"], ch["pw"])
        for ch in chains:
            ch["inv"] = ch["inv"] + _dot3(ch["inv"], ch["pw"])
    for ch in chains:
        ch["inv_b"] = ch["inv"].astype(bf)
        ch["u"] = _dot(ch["inv_b"], ch["vb_b"])
    for ch in chains:
        ch["w_b"] = _dot(ch["inv_b"], ch["kbe_b"]).astype(bf)
    for ch in chains:
        ch["state_b"] = state_ref[ch["idx"]].astype(bf)
        ch["vnew_b"] = (ch["u"] - _dot(ch["w_b"], ch["state_b"])).astype(bf)
    for ch in chains:
        ch["out"] = _dot(ch["qdec_b"], ch["state_b"]) + _dot(ch["attn_b"], ch["vnew_b"])
    for ch in chains:
        ch["kdec_t"] = _nt_dot(eye_d, ch["kdec_b"]).astype(bf)
    for ch in chains:
        state_ref[ch["idx"]] = state_ref[ch["idx"]] * ch["e_last"] + _dot(ch["kdec_t"], ch["vnew_b"])
    for ch in chains:
        out = ch["out"]
        on = out * lax.rsqrt(jnp.mean(out * out, axis=-1, keepdims=True) + EPS) * gain
        z = z_ref[ch["b"], :, ch["cols"]]
        o_ref[ch["b"], :, ch["cols"]] = (on * (z * jax.nn.sigmoid(z))).astype(o_ref.dtype)


def deltanet_mixer(proj, q, k, v, g, beta, out_gain, batch, seq_len):
    w = BRANCH_W
    cs = B_CHUNK
    r3 = lambda a: a.reshape(batch, seq_len, a.shape[-1])
    blk = lambda width, cb=0: pl.BlockSpec((batch, cs, width), lambda c: (0, c, cb))
    out = pl.pallas_call(
        _delta_kernel,
        grid=(seq_len // cs,),
        in_specs=[blk(w), blk(w), blk(w), blk(B_HEADS), blk(B_HEADS), blk(w, P_BZ // w),
                  pl.BlockSpec((1, HEAD_DIM), lambda c: (0, 0))],
        out_specs=blk(w),
        out_shape=jax.ShapeDtypeStruct((batch, seq_len, w), MXU_DTYPE),
        scratch_shapes=[pltpu.VMEM((batch * B_HEADS, HEAD_DIM, HEAD_DIM), jnp.float32)],
        compiler_params=_cparams("arbitrary"),
        name="mixer_b",
    )(r3(q), r3(k), r3(v), r3(g), r3(beta), r3(proj), out_gain.reshape(1, HEAD_DIM))
    return out.reshape(batch * seq_len, w)


def _merge_kernel(oa_ref, ob_ref, oc_ref, od_ref, g_ref, wb_ref, wg_ref, o_ref):
    lat = g_ref[...].astype(MXU_DTYPE)
    acc = None
    for n, o in enumerate((oa_ref, ob_ref, oc_ref, od_ref)):
        y = _dot(o[...], wb_ref[n])
        gate = jax.nn.sigmoid(_dot(lat, wg_ref[n]))
        acc = gate * y if acc is None else acc + gate * y
    o_ref[...] = acc.astype(o_ref.dtype)


def merge_branches(outs, proj, w_branch, w_gate, layer, tm=1024, tn=512):
    t = proj.shape[0]
    w = BRANCH_W
    tm = min(tm, t)
    n_br = len(outs)
    return pl.pallas_call(
        _merge_kernel,
        grid=(t // tm, D_MODEL // tn),
        in_specs=[pl.BlockSpec((tm, w), lambda i, j: (i, 0))] * n_br
        + [pl.BlockSpec((tm, GATE_RANK), lambda i, j: (i, P_G // GATE_RANK)),
           pl.BlockSpec((None, n_br, w, tn), lambda i, j: (layer, 0, 0, j)),
           pl.BlockSpec((None, n_br, GATE_RANK, tn), lambda i, j: (layer, 0, 0, j))],
        out_specs=pl.BlockSpec((tm, tn), lambda i, j: (i, j)),
        out_shape=jax.ShapeDtypeStruct((t, D_MODEL), MXU_DTYPE),
        compiler_params=_cparams("parallel", "parallel"),
        name="merge",
    )(*outs, proj, w_branch, w_gate)


def _permute_w_in(w):
    z = jnp.zeros(w.shape[:-1] + (N_P - 11872,), w.dtype)
    return jnp.concatenate([
        w[..., 0:1024],
        w[..., 1280:2304],
        w[..., 2384:5456],
        w[..., 5456:6480],
        w[..., 6496:8544],
        w[..., 8544:11616],
        w[..., 11616:11872],
        w[..., 1024:1152],
        w[..., 1152:1280],
        w[..., 2304:2368],
        w[..., 2368:2384],
        w[..., 6480:6488],
        w[..., 6488:6496],
        z], axis=-1)


def kernel(x, mix_norm, w_in, a_q_norm, a_k_norm, a_idx_k_norm, b_conv, b_a_log, b_dt_bias, b_out_norm,
           c_ln_gain, c_ln_bias, c_spatial_w, c_spatial_b, d_conv, w_branch, w_gate_up, w_out, ffn_norm,
           w_ff1, w_ff2):
    bn, seq_len, d = x.shape
    depth = w_in.shape[0]
    bf = MXU_DTYPE
    w_in_p = _permute_w_in(w_in.astype(bf))
    w_branch_b = w_branch.astype(bf).reshape(depth, 4, BRANCH_W, d)
    w_gate_b = w_gate_up.astype(bf)
    w_out_b = w_out.astype(bf)
    w_ff1_b = w_ff1.astype(bf)
    w_ff2_b = w_ff2.astype(bf)

    xt = x.reshape(bn * seq_len, d)
    for l in range(depth):
        h = rmsnorm_rows(xt, mix_norm[l])
        proj = matmul(h, w_in_p, l)
        kn, vt, ki = dsa_prep(proj, a_k_norm[l], a_idx_k_norm[l])
        o_a = dsa_mixer(proj, kn, vt, ki, a_q_norm[l], bn, seq_len)
        bq, bk, bv, bg, bbeta = deltanet_prep(proj, b_conv[l], b_a_log[l], b_dt_bias[l], seq_len)
        o_b = deltanet_mixer(proj, bq, bk, bv, bg, bbeta, b_out_norm[l], bn, seq_len)
        o_c = spatial_gating_mixer(proj, c_ln_gain[l], c_ln_bias[l], c_spatial_w[l], c_spatial_b[l])
        o_d = short_conv_mixer(proj, d_conv[l], seq_len)
        merged = merge_branches((o_a, o_b, o_c, o_d), proj, w_branch_b, w_gate_b, l)
        xt = matmul(merged, w_out_b, l, epilogue="residual", residual=xt, tn=512)
        f = rmsnorm_rows(xt, ffn_norm[l])
        mid = matmul(f, w_ff1_b, l, epilogue="relu2", out_dtype=bf)
        xt = matmul(mid, w_ff2_b, l, epilogue="residual", residual=xt, tn=512)
    return xt.reshape(bn, seq_len, d)
```

```python
import functools

import jax
import jax.numpy as jnp
from jax import lax
from jax.experimental import pallas as pl
from jax.experimental.pallas import tpu as pltpu

D_MODEL = 4096
HEAD_DIM = 128
BRANCH_W = 1024
A_HEADS = 8
IDX_HEADS = 16
IDX_DIM = 64
IDX_W_SCALE = (IDX_HEADS * IDX_DIM) ** -0.5
TOPK_MAX = 256
Q_BLOCK = 128
B_HEADS = 8
B_CONV = 4
B_CHUNK = 64
C_CHUNK = 128
C_GROUPS = 8
D_CONV = 3
GATE_RANK = 256
EPS = 1e-6

P_AQ, P_IQ, P_BQ, P_BZ, P_CU, P_CV, P_DH, P_DB, P_DC = 0, 1024, 2048, 5120, 6144, 7168, 8192, 9216, 10240
P_G, P_AK, P_AV, P_SM = 11264, 11520, 11648, 11776
N_P = 11904
SM_IK, SM_IW, SM_BA, SM_BB = 0, 64, 80, 88

MXU_DTYPE = jnp.bfloat16
HALO = 8
VMEM_LIMIT = 56 * 1024 * 1024
NEG_BIG = -1e30
INT_MIN = -(2 ** 31)
KEY_NEG_INF = INT_MIN + 0x7FFFFF
HIGHEST = lax.Precision.HIGHEST


def _cparams(*sem):
    return pltpu.CompilerParams(dimension_semantics=sem, vmem_limit_bytes=VMEM_LIMIT)


def _nt_dot(a, b, precision=None):
    return lax.dot_general(a, b, (((1,), (1,)), ((), ())), precision=precision,
                           preferred_element_type=jnp.float32)


def _dot(a, b, precision=None):
    return jnp.dot(a, b, precision=precision, preferred_element_type=jnp.float32)


def _dot3(a, b):
    a_hi, b_hi = a.astype(MXU_DTYPE), b.astype(MXU_DTYPE)
    a_lo = (a - a_hi.astype(jnp.float32)).astype(MXU_DTYPE)
    b_lo = (b - b_hi.astype(jnp.float32)).astype(MXU_DTYPE)
    return _dot(a_hi, b_hi) + (_dot(a_hi, b_lo) + _dot(a_lo, b_hi))


def _eye(n, dtype):
    r = lax.broadcasted_iota(jnp.int32, (n, n), 0)
    c = lax.broadcasted_iota(jnp.int32, (n, n), 1)
    return (r == c).astype(dtype)


def _rmsnorm_kernel(x_ref, g_ref, o_ref):
    x = x_ref[...]
    ms = jnp.mean(x * x, axis=-1, keepdims=True)
    o_ref[...] = (x * lax.rsqrt(ms + EPS) * g_ref[...]).astype(o_ref.dtype)


def rmsnorm_rows(x, gain, tl=256):
    t, d = x.shape
    return pl.pallas_call(
        _rmsnorm_kernel,
        grid=(t // tl,),
        in_specs=[pl.BlockSpec((tl, d), lambda i: (i, 0)),
                  pl.BlockSpec((1, d), lambda i: (0, 0))],
        out_specs=pl.BlockSpec((tl, d), lambda i: (i, 0)),
        out_shape=jax.ShapeDtypeStruct((t, d), MXU_DTYPE),
        compiler_params=_cparams("parallel"),
        name="rmsnorm",
    )(x, gain.reshape(1, d))


def _mm_kernel(a_ref, w_ref, *rest, nk, epilogue):
    if epilogue == "residual":
        r_ref, o_ref = rest[0], rest[1]
        scratch = rest[2:]
    else:
        r_ref, o_ref = None, rest[0]
        scratch = rest[1:]
    part = _dot(a_ref[...], w_ref[...])

    def finish(acc):
        if epilogue == "residual":
            o_ref[...] = r_ref[...] + acc
        elif epilogue == "relu2":
            o_ref[...] = jnp.square(jnp.maximum(acc, 0.0)).astype(o_ref.dtype)
        else:
            o_ref[...] = acc.astype(o_ref.dtype)

    if nk == 1:
        finish(part)
    else:
        acc_ref = scratch[0]
        k = pl.program_id(2)

        @pl.when(k == 0)
        def _():
            acc_ref[...] = part

        @pl.when(k > 0)
        def _():
            acc_ref[...] += part

        @pl.when(k == nk - 1)
        def _():
            finish(acc_ref[...])


def matmul(a, w_stack, layer, *, epilogue="none", residual=None, out_dtype=jnp.float32, tm=1024, tn=1024, tk=4096):
    m, kdim = a.shape
    n = w_stack.shape[2]
    tm, tn, tk = min(tm, m), min(tn, n), min(tk, kdim)
    nk = kdim // tk
    in_specs = [pl.BlockSpec((tm, tk), lambda i, j, k: (i, k)),
                pl.BlockSpec((None, tk, tn), lambda i, j, k: (layer, k, j))]
    args = [a, w_stack]
    if epilogue == "residual":
        in_specs.append(pl.BlockSpec((tm, tn), lambda i, j, k: (i, j)))
        args.append(residual)
    scratch = [pltpu.VMEM((tm, tn), jnp.float32)] if nk > 1 else []
    return pl.pallas_call(
        functools.partial(_mm_kernel, nk=nk, epilogue=epilogue),
        grid=(m // tm, pl.cdiv(n, tn), nk),
        in_specs=in_specs,
        out_specs=pl.BlockSpec((tm, tn), lambda i, j, k: (i, j)),
        out_shape=jax.ShapeDtypeStruct((m, n), out_dtype),
        scratch_shapes=scratch,
        compiler_params=_cparams("parallel", "parallel", "arbitrary"),
        name="matmul_" + epilogue,
    )(*args)


def _dconv_kernel(h_ref, b_ref, c_ref, hp_ref, cp_ref, w_ref, o_ref, ext_ref, *, tiles_per_seq):
    tl = h_ref.shape[0]
    first = (pl.program_id(0) % tiles_per_seq) == 0
    ext_ref[0:HALO, :] = jnp.where(first, 0.0, cp_ref[...] * hp_ref[...])
    p = c_ref[...] * h_ref[...]
    ext_ref[HALO:HALO + tl, :] = p
    w = w_ref[...]
    acc = w[2:3, :] * p
    acc = acc + w[1:2, :] * ext_ref[HALO - 1:HALO - 1 + tl, :]
    acc = acc + w[0:1, :] * ext_ref[HALO - 2:HALO - 2 + tl, :]
    o_ref[...] = (b_ref[...] * acc).astype(o_ref.dtype)


def short_conv_mixer(proj, conv_w, seq_len, tl=512):
    t = proj.shape[0]
    w = BRANCH_W
    cur = lambda cb: pl.BlockSpec((tl, w), lambda i: (i, cb))
    prev = lambda cb: pl.BlockSpec((HALO, w), lambda i: (jnp.maximum(i * (tl // HALO) - 1, 0), cb))
    return pl.pallas_call(
        functools.partial(_dconv_kernel, tiles_per_seq=seq_len // tl),
        grid=(t // tl,),
        in_specs=[cur(P_DH // w), cur(P_DB // w), cur(P_DC // w), prev(P_DH // w), prev(P_DC // w),
                  pl.BlockSpec((D_CONV, w), lambda i: (0, 0))],
        out_specs=pl.BlockSpec((tl, w), lambda i: (i, 0)),
        out_shape=jax.ShapeDtypeStruct((t, w), MXU_DTYPE),
        scratch_shapes=[pltpu.VMEM((tl + HALO, w), jnp.float32)],
        compiler_params=_cparams("parallel"),
        name="mixer_d",
    )(proj, proj, proj, proj, proj, conv_w)


def _gmlp_kernel(u_ref, v_ref, lg_ref, lb_ref, ws_ref, bs_ref, o_ref):
    tl = u_ref.shape[0]
    gd = BRANCH_W // C_GROUPS
    row = lax.broadcasted_iota(jnp.int32, (C_CHUNK, C_CHUNK), 0)
    col = lax.broadcasted_iota(jnp.int32, (C_CHUNK, C_CHUNK), 1)
    causal = row >= col
    for g in range(C_GROUPS):
        w_m = jnp.where(causal, ws_ref[g], 0.0).astype(MXU_DTYPE)
        bias = bs_ref[:, g:g + 1]
        gain = lg_ref[:, g * gd:(g + 1) * gd]
        shift = lb_ref[:, g * gd:(g + 1) * gd]
        for c in range(tl // C_CHUNK):
            rows = slice(c * C_CHUNK, (c + 1) * C_CHUNK)
            cols = slice(g * gd, (g + 1) * gd)
            v = jax.nn.gelu(v_ref[rows, cols])
            mu = jnp.mean(v, axis=-1, keepdims=True)
            vc = v - mu
            var = jnp.mean(vc * vc, axis=-1, keepdims=True)
            vn = vc * lax.rsqrt(var + EPS) * gain + shift
            s = _dot(w_m, vn.astype(MXU_DTYPE)) + bias
            o_ref[rows, cols] = (jax.nn.gelu(u_ref[rows, cols]) * s).astype(o_ref.dtype)


def spatial_gating_mixer(proj, ln_gain, ln_bias, w_s, b_s, tl=256):
    t = proj.shape[0]
    w = BRANCH_W
    return pl.pallas_call(
        _gmlp_kernel,
        grid=(t // tl,),
        in_specs=[pl.BlockSpec((tl, w), lambda i: (i, P_CU // w)),
                  pl.BlockSpec((tl, w), lambda i: (i, P_CV // w)),
                  pl.BlockSpec((1, w), lambda i: (0, 0)),
                  pl.BlockSpec((1, w), lambda i: (0, 0)),
                  pl.BlockSpec((C_GROUPS, C_CHUNK, C_CHUNK), lambda i: (0, 0, 0)),
                  pl.BlockSpec((C_CHUNK, C_GROUPS), lambda i: (0, 0))],
        out_specs=pl.BlockSpec((tl, w), lambda i: (i, 0)),
        out_shape=jax.ShapeDtypeStruct((t, w), MXU_DTYPE),
        compiler_params=_cparams("parallel"),
        name="mixer_c",
    )(proj, proj, ln_gain.reshape(1, w), ln_bias.reshape(1, w), w_s, b_s.T)


def _a_prep_kernel(k_ref, v_ref, sm_ref, gk_ref, gi_ref, kn_ref, vt_ref, ki_ref):
    k = k_ref[...]
    kn = k * lax.rsqrt(jnp.mean(k * k, axis=-1, keepdims=True) + EPS) * gk_ref[...]
    kn_ref[...] = kn.astype(kn_ref.dtype)
    vt_ref[...] = _nt_dot(_eye(HEAD_DIM, MXU_DTYPE), v_ref[...].astype(MXU_DTYPE)).astype(vt_ref.dtype)
    ik = sm_ref[:, SM_IK:SM_IK + IDX_DIM]
    ikn = ik * lax.rsqrt(jnp.mean(ik * ik, axis=-1, keepdims=True) + EPS) * gi_ref[...]
    ki_ref[...] = ikn.astype(ki_ref.dtype)


def dsa_prep(proj, k_gain, ik_gain, tl=512):
    t = proj.shape[0]
    hd = HEAD_DIM
    return pl.pallas_call(
        _a_prep_kernel,
        grid=(t // tl,),
        in_specs=[pl.BlockSpec((tl, hd), lambda i: (i, P_AK // hd)),
                  pl.BlockSpec((tl, hd), lambda i: (i, P_AV // hd)),
                  pl.BlockSpec((tl, hd), lambda i: (i, P_SM // hd)),
                  pl.BlockSpec((1, hd), lambda i: (0, 0)),
                  pl.BlockSpec((1, IDX_DIM), lambda i: (0, 0))],
        out_specs=[pl.BlockSpec((tl, hd), lambda i: (i, 0)),
                   pl.BlockSpec((hd, tl), lambda i: (0, i)),
                   pl.BlockSpec((tl, IDX_DIM), lambda i: (i, 0))],
        out_shape=[jax.ShapeDtypeStruct((t, hd), MXU_DTYPE),
                   jax.ShapeDtypeStruct((hd, t), MXU_DTYPE),
                   jax.ShapeDtypeStruct((t, IDX_DIM), MXU_DTYPE)],
        compiler_params=_cparams("parallel"),
        name="mixer_a_prep",
    )(proj, proj, proj, k_gain.reshape(1, hd), ik_gain.reshape(1, IDX_DIM))


def _dsa_kernel(q_ref, iq_ref, sm_ref, kn_ref, vt_ref, ki_ref, gq_ref, o_ref,
                keys_ref, hi_ref, lo_ref, qt_ref, qit_ref, m_ref, l_ref, acc_ref, raw_ref, p_ref,
                *, topk, ck, seq_len):
    qb = Q_BLOCK
    bf = MXU_DTYPE
    f32 = jnp.float32
    i = pl.program_id(1)
    n_chunks = (i * qb + qb + ck - 1) // ck

    def chunk_start(c):
        return pl.multiple_of(c * ck, ck)

    def col_sum(x):
        return jnp.sum(jnp.sum(x.reshape(4, ck // 32, 8, qb), axis=1), axis=0)

    gq = gq_ref[...]
    eye_d = _eye(HEAD_DIM, bf)
    for h in range(A_HEADS):
        qh = q_ref[:, h * HEAD_DIM:(h + 1) * HEAD_DIM]
        qn = qh * lax.rsqrt(jnp.mean(qh * qh, axis=-1, keepdims=True) + EPS) * gq
        qt_ref[:, h * qb:(h + 1) * qb] = _nt_dot(eye_d, qn.astype(bf)).astype(bf)
    eye_i = _eye(IDX_DIM, bf)
    for h in range(IDX_HEADS):
        qi = iq_ref[:, h * IDX_DIM:(h + 1) * IDX_DIM].astype(bf)
        qit_ref[:, h * qb:(h + 1) * qb] = _nt_dot(eye_i, qi).astype(bf)
    w_t = _nt_dot(_eye(IDX_HEADS, f32), sm_ref[:, SM_IW:SM_IW + IDX_HEADS] * IDX_W_SCALE,
                  precision=HIGHEST)

    q_pos = i * qb + lax.broadcasted_iota(jnp.int32, (ck, qb), 1)
    k_off = lax.broadcasted_iota(jnp.int32, (ck, qb), 0)

    def score_chunk(c, carry):
        start = chunk_start(c)
        kc = ki_ref[pl.ds(start, ck), :]
        acc = jnp.zeros((ck, qb), f32)
        for hp in range(IDX_HEADS // 2):
            r = _dot(kc, qit_ref[:, hp * 2 * qb:(hp + 1) * 2 * qb])
            for s in range(2):
                h = 2 * hp + s
                acc = acc + w_t[h:h + 1, :] * jnp.maximum(r[:, s * qb:(s + 1) * qb], 0.0)
        score = jnp.where(k_off + start <= q_pos, acc, -jnp.inf)
        bits = pltpu.bitcast(score, jnp.int32)
        key = bits ^ ((bits >> 31) & 0x7FFFFFFF)
        keys_ref[pl.ds(start, ck), :] = key
        hi_ref[pl.ds(start, ck), :] = (key >> 16).astype(jnp.int16)
        return carry

    lax.fori_loop(0, n_chunks, score_chunk, 0)

    def count(pred):
        def body(c, cnt):
            start = chunk_start(c)
            hit = pred(keys_ref[pl.ds(start, ck), :], k_off + start)
            return cnt + col_sum(jnp.where(hit, 1.0, 0.0))
        cnt = lax.fori_loop(0, n_chunks, body, jnp.zeros((8, qb), f32))
        return jnp.sum(cnt, axis=0, keepdims=True)

    def count16(ref, cand16, strict):
        one, zero = jnp.bfloat16(1), jnp.bfloat16(0)

        def hits(c):
            vals = ref[pl.ds(chunk_start(c), ck), :]
            hit = jnp.where(vals > cand16 if strict else vals >= cand16, one, zero)
            parts = [hit[r * 16:(r + 1) * 16] for r in range(ck // 16)]
            while len(parts) > 1:
                parts = [parts[r] + parts[r + 1] for r in range(0, len(parts), 2)]
            return parts[0]

        def two_chunks(j, cnt):
            return cnt + (hits(2 * j) + hits(2 * j + 1)).astype(f32)

        cnt = lax.fori_loop(0, n_chunks // 2, two_chunks, jnp.zeros((16, qb), f32))
        odd = (n_chunks % 2).astype(f32)
        cnt = cnt + hits(n_chunks - 1).astype(f32) * odd
        return jnp.sum(cnt, axis=0, keepdims=True)

    def search16(ref, want):
        def bit_step(b, u):
            cand = u | lax.shift_left(jnp.int32(1), 15 - b)
            total = count16(ref, (cand - 32768).astype(jnp.int16), False)
            return jnp.where(total >= want, cand, u)
        return lax.fori_loop(0, 16, bit_step, jnp.zeros((1, qb), jnp.int32))

    tau_hi = search16(hi_ref, float(topk)) - 32768
    tau_hi16 = tau_hi.astype(jnp.int16)
    want_lo = float(topk) - count16(hi_ref, tau_hi16, True)

    def build_lo(c, carry):
        start = chunk_start(c)
        low = ((keys_ref[pl.ds(start, ck), :] & 0xFFFF) - 32768).astype(jnp.int16)
        lo_ref[pl.ds(start, ck), :] = jnp.where(hi_ref[pl.ds(start, ck), :] == tau_hi16, low, jnp.int16(-32768))
        return carry

    lax.fori_loop(0, n_chunks, build_lo, 0)
    tau = lax.shift_left(tau_hi, 16) | search16(lo_ref, want_lo)
    tau = jnp.maximum(tau, KEY_NEG_INF + 1)

    need = float(topk) - count(lambda kk, _: kk > tau)
    n_equal = count(lambda kk, _: kk == tau)
    has_tie = jnp.max(jnp.where(n_equal > need, 1.0, 0.0)) > 0.5

    @pl.when(has_tie)
    def _():
        nbits = seq_len.bit_length()

        def tie_bit(b, x):
            cand = x | lax.shift_left(jnp.int32(1), nbits - 1 - b)
            below = count(lambda kk, kidx: jnp.where(kk == tau, kidx, cand) < cand)
            return jnp.where(below < need, cand, x)

        x = lax.fori_loop(0, nbits, tie_bit, jnp.zeros((1, qb), jnp.int32))

        def demote(c, carry):
            start = chunk_start(c)
            kk = keys_ref[pl.ds(start, ck), :]
            drop = jnp.where(kk == tau, k_off + start, x) > x
            keys_ref[pl.ds(start, ck), :] = jnp.where(drop, KEY_NEG_INF, kk)
            return carry

        lax.fori_loop(0, n_chunks, demote, 0)

    m_ref[...] = jnp.full(m_ref.shape, NEG_BIG, f32)
    l_ref[...] = jnp.zeros(l_ref.shape, f32)
    acc_ref[...] = jnp.zeros(acc_ref.shape, f32)
    c_exp = (HEAD_DIM ** -0.5) * 1.4426950408889634
    pairs = [slice(hp * 2 * qb, (hp + 1) * 2 * qb) for hp in range(A_HEADS // 2)]

    def attn_chunk(c, carry):
        start = chunk_start(c)
        kc = kn_ref[pl.ds(start, ck), :]
        vt = vt_ref[:, pl.ds(start, ck)]
        for cols in pairs:
            raw_ref[:, cols] = _dot(kc, qt_ref[:, cols])
        rb = 128
        for cols in pairs:
            top = None
            for r in range(ck // rb):
                sel = keys_ref[pl.ds(start + r * rb, rb), :] >= tau
                s = raw_ref[r * rb:(r + 1) * rb, cols]
                s = jnp.concatenate([jnp.where(sel, s[:, :qb], NEG_BIG), jnp.where(sel, s[:, qb:], NEG_BIG)], axis=1)
                raw_ref[r * rb:(r + 1) * rb, cols] = s
                blk = jnp.max(s.reshape(rb // 8, 8, 2 * qb), axis=0)
                top = blk if top is None else jnp.maximum(top, blk)
            m_old = m_ref[:, cols]
            m_new = jnp.maximum(m_old, jnp.max(top, axis=0, keepdims=True))
            alpha = jnp.exp2((m_old - m_new) * c_exp)
            tot = jnp.zeros((8, 2 * qb), f32)
            for r in range(ck // rb):
                p = jnp.exp2((raw_ref[r * rb:(r + 1) * rb, cols] - m_new) * c_exp)
                tot = tot + jnp.sum(p.reshape(rb // 8, 8, 2 * qb), axis=0)
                p_ref[r * rb:(r + 1) * rb, cols] = p.astype(bf)
            l_ref[:, cols] = alpha * l_ref[:, cols] + jnp.sum(tot, axis=0, keepdims=True)
            m_ref[:, cols] = m_new
            acc_ref[:, cols] = alpha * acc_ref[:, cols] + _dot(vt, p_ref[:, cols])
        return carry

    lax.fori_loop(0, n_chunks, attn_chunk, 0)

    for h in range(A_HEADS):
        cols = slice(h * qb, (h + 1) * qb)
        o_t = acc_ref[:, cols] / l_ref[:, cols]
        o_ref[:, h * HEAD_DIM:(h + 1) * HEAD_DIM] = o_t.T.astype(o_ref.dtype)


def dsa_mixer(proj, kn, vt, ki, q_gain, batch, seq_len, ck=512):
    t = proj.shape[0]
    qb = Q_BLOCK
    nq = seq_len // qb
    ck = min(ck, seq_len)
    topk = min(TOPK_MAX, seq_len // 4)
    w = BRANCH_W
    return pl.pallas_call(
        functools.partial(_dsa_kernel, topk=topk, ck=ck, seq_len=seq_len),
        grid=(batch, nq),
        in_specs=[pl.BlockSpec((qb, w), lambda b, i: (b * nq + i, P_AQ // w)),
                  pl.BlockSpec((qb, w), lambda b, i: (b * nq + i, P_IQ // w)),
                  pl.BlockSpec((qb, HEAD_DIM), lambda b, i: (b * nq + i, P_SM // HEAD_DIM)),
                  pl.BlockSpec((seq_len, HEAD_DIM), lambda b, i: (b, 0)),
                  pl.BlockSpec((HEAD_DIM, seq_len), lambda b, i: (0, b)),
                  pl.BlockSpec((seq_len, IDX_DIM), lambda b, i: (b, 0)),
                  pl.BlockSpec((1, HEAD_DIM), lambda b, i: (0, 0))],
        out_specs=pl.BlockSpec((qb, w), lambda b, i: (b * nq + i, 0)),
        out_shape=jax.ShapeDtypeStruct((t, w), MXU_DTYPE),
        scratch_shapes=[pltpu.VMEM((seq_len, qb), jnp.int32),
                        pltpu.VMEM((seq_len, qb), jnp.int16),
                        pltpu.VMEM((seq_len, qb), jnp.int16),
                        pltpu.VMEM((HEAD_DIM, A_HEADS * qb), MXU_DTYPE),
                        pltpu.VMEM((IDX_DIM, IDX_HEADS * qb), MXU_DTYPE),
                        pltpu.VMEM((1, A_HEADS * qb), jnp.float32),
                        pltpu.VMEM((1, A_HEADS * qb), jnp.float32),
                        pltpu.VMEM((HEAD_DIM, A_HEADS * qb), jnp.float32),
                        pltpu.VMEM((ck, A_HEADS * qb), jnp.float32),
                        pltpu.VMEM((ck, A_HEADS * qb), MXU_DTYPE)],
        compiler_params=_cparams("parallel", "arbitrary"),
        name="mixer_a",
    )(proj, proj, proj, kn, vt, ki, q_gain.reshape(1, HEAD_DIM))


def _b_prep_kernel(q_ref, k_ref, v_ref, qp_ref, kp_ref, vp_ref, sm_ref, cw_ref, alog_ref, dtb_ref,
                   qo_ref, ko_ref, vo_ref, g_ref, beta_ref, ext_ref, *, tiles_per_seq):
    tl = q_ref.shape[0]
    w = BRANCH_W
    first = (pl.program_id(0) % tiles_per_seq) == 0
    parts = ((q_ref, qp_ref, qo_ref, True), (k_ref, kp_ref, ko_ref, True), (v_ref, vp_ref, vo_ref, False))
    for n, (cur, prev, out, normed) in enumerate(parts):
        ext_ref[0:HALO, :] = jnp.where(first, 0.0, prev[...])
        ext_ref[HALO:HALO + tl, :] = cur[...]
        cw = cw_ref[:, n * w:(n + 1) * w]
        acc = cw[B_CONV - 1:B_CONV, :] * cur[...]
        for j in range(B_CONV - 1):
            off = HALO - (B_CONV - 1) + j
            acc = acc + cw[j:j + 1, :] * ext_ref[off:off + tl, :]
        y = acc * jax.nn.sigmoid(acc)
        if normed:
            for h in range(B_HEADS):
                cols = slice(h * HEAD_DIM, (h + 1) * HEAD_DIM)
                yh = y[:, cols]
                out[:, cols] = yh * lax.rsqrt(jnp.sum(yh * yh, axis=-1, keepdims=True) + EPS)
        else:
            out[...] = y
    a = sm_ref[:, SM_BA:SM_BA + B_HEADS] + dtb_ref[...]
    softplus = jnp.maximum(a, 0.0) + jnp.log(1.0 + jnp.exp(-jnp.abs(a)))
    g_ref[...] = -jnp.exp(alog_ref[...]) * softplus
    beta_ref[...] = jax.nn.sigmoid(sm_ref[:, SM_BB:SM_BB + B_HEADS])


def deltanet_prep(proj, conv_w, a_log, dt_bias, seq_len, tl=256):
    t = proj.shape[0]
    w = BRANCH_W
    cb = P_BQ // w
    cur = lambda c: pl.BlockSpec((tl, w), lambda i: (i, c))
    prev = lambda c: pl.BlockSpec((HALO, w), lambda i: (jnp.maximum(i * (tl // HALO) - 1, 0), c))
    row = lambda width: pl.BlockSpec((tl, width), lambda i: (i, 0))
    return pl.pallas_call(
        functools.partial(_b_prep_kernel, tiles_per_seq=seq_len // tl),
        grid=(t // tl,),
        in_specs=[cur(cb), cur(cb + 1), cur(cb + 2), prev(cb), prev(cb + 1), prev(cb + 2),
                  pl.BlockSpec((tl, HEAD_DIM), lambda i: (i, P_SM // HEAD_DIM)),
                  pl.BlockSpec((B_CONV, 3 * w), lambda i: (0, 0)),
                  pl.BlockSpec((1, B_HEADS), lambda i: (0, 0)),
                  pl.BlockSpec((1, B_HEADS), lambda i: (0, 0))],
        out_specs=[row(w), row(w), row(w), row(B_HEADS), row(B_HEADS)],
        out_shape=[jax.ShapeDtypeStruct((t, w), jnp.float32)] * 3
        + [jax.ShapeDtypeStruct((t, B_HEADS), jnp.float32)] * 2,
        scratch_shapes=[pltpu.VMEM((tl + HALO, w), jnp.float32)],
        compiler_params=_cparams("parallel"),
        name="mixer_b_prep",
    )(proj, proj, proj, proj, proj, proj, proj, conv_w, a_log.reshape(1, B_HEADS), dt_bias.reshape(1, B_HEADS))


def _delta_kernel(q_ref, k_ref, v_ref, g_ref, beta_ref, z_ref, gain_ref, o_ref, state_ref):
    nb = q_ref.shape[0]
    cs = B_CHUNK
    bf = MXU_DTYPE
    f32 = jnp.float32

    @pl.when(pl.program_id(0) == 0)
    def _():
        state_ref[...] = jnp.zeros(state_ref.shape, f32)

    row = lax.broadcasted_iota(jnp.int32, (cs, cs), 0)
    col = lax.broadcasted_iota(jnp.int32, (cs, cs), 1)
    incl = row >= col
    strict = row > col
    eye_c = (row == col).astype(f32)
    eye_h = _eye(B_HEADS, f32)
    eye_d = _eye(HEAD_DIM, bf)
    tril = incl.astype(f32)
    gain = gain_ref[...]
    qscale = HEAD_DIM ** -0.5

    chains = []
    for b in range(nb):
        gc = _dot(tril, g_ref[b], precision=HIGHEST)
        gc_t = _nt_dot(eye_h, gc, precision=HIGHEST)
        eg = jnp.exp(gc)
        g_last = gc[cs - 1:cs, :]
        ek = jnp.exp(g_last - gc)
        e_last = jnp.exp(g_last)
        beta = beta_ref[b]
        for h in range(B_HEADS):
            cols = slice(h * HEAD_DIM, (h + 1) * HEAD_DIM)
            qh = q_ref[b, :, cols] * qscale
            kh = k_ref[b, :, cols]
            bh = beta[:, h:h + 1]
            kb = kh * bh
            decay = jnp.exp(jnp.where(incl, gc[:, h:h + 1] - gc_t[h:h + 1, :], -jnp.inf))
            chains.append(dict(
                b=b, h=h, cols=cols, idx=b * B_HEADS + h, decay=decay,
                q_b=qh.astype(bf), k_b=kh.astype(bf), kb_b=kb.astype(bf),
                vb_b=(v_ref[b, :, cols] * bh).astype(bf),
                kbe_b=(kb * eg[:, h:h + 1]).astype(bf),
                qdec_b=(qh * eg[:, h:h + 1]).astype(bf),
                kdec_b=(kh * ek[:, h:h + 1]).astype(bf),
                e_last=e_last[:, h:h + 1]))

    for ch in chains:
        ch["lower"] = jnp.where(strict, _nt_dot(ch["kb_b"], ch["k_b"]) * ch["decay"], 0.0)
    for ch in chains:
        ch["attn_b"] = jnp.where(incl, _nt_dot(ch["q_b"], ch["k_b"]) * ch["decay"], 0.0).astype(bf)
    for ch in chains:
        ch["inv"] = eye_c - ch["lower"]
        ch["pw"] = ch["lower"]
    for _ in range(5):
        for ch in chains:
            ch["pw"] = _dot3(ch["pw"], ch["pw"])
        for ch in chains:
            ch["inv"] = ch["inv"] + _dot3(ch["inv"], ch["pw"])
    for ch in chains:
        ch["inv_b"] = ch["inv"].astype(bf)
        ch["u"] = _dot(ch["inv_b"], ch["vb_b"])
    for ch in chains:
        ch["w_b"] = _dot(ch["inv_b"], ch["kbe_b"]).astype(bf)
    for ch in chains:
        ch["state_b"] = state_ref[ch["idx"]].astype(bf)
        ch["vnew_b"] = (ch["u"] - _dot(ch["w_b"], ch["state_b"])).astype(bf)
    for ch in chains:
        ch["out"] = _dot(ch["qdec_b"], ch["state_b"]) + _dot(ch["attn_b"], ch["vnew_b"])
    for ch in chains:
        ch["kdec_t"] = _nt_dot(eye_d, ch["kdec_b"]).astype(bf)
    for ch in chains:
        state_ref[ch["idx"]] = state_ref[ch["idx"]] * ch["e_last"] + _dot(ch["kdec_t"], ch["vnew_b"])
    for ch in chains:
        out = ch["out"]
        on = out * lax.rsqrt(jnp.mean(out * out, axis=-1, keepdims=True) + EPS) * gain
        z = z_ref[ch["b"], :, ch["cols"]]
        o_ref[ch["b"], :, ch["cols"]] = (on * (z * jax.nn.sigmoid(z))).astype(o_ref.dtype)


def deltanet_mixer(proj, q, k, v, g, beta, out_gain, batch, seq_len):
    w = BRANCH_W
    cs = B_CHUNK
    r3 = lambda a: a.reshape(batch, seq_len, a.shape[-1])
    blk = lambda width, cb=0: pl.BlockSpec((batch, cs, width), lambda c: (0, c, cb))
    out = pl.pallas_call(
        _delta_kernel,
        grid=(seq_len // cs,),
        in_specs=[blk(w), blk(w), blk(w), blk(B_HEADS), blk(B_HEADS), blk(w, P_BZ // w),
                  pl.BlockSpec((1, HEAD_DIM), lambda c: (0, 0))],
        out_specs=blk(w),
        out_shape=jax.ShapeDtypeStruct((batch, seq_len, w), MXU_DTYPE),
        scratch_shapes=[pltpu.VMEM((batch * B_HEADS, HEAD_DIM, HEAD_DIM), jnp.float32)],
        compiler_params=_cparams("arbitrary"),
        name="mixer_b",
    )(r3(q), r3(k), r3(v), r3(g), r3(beta), r3(proj), out_gain.reshape(1, HEAD_DIM))
    return out.reshape(batch * seq_len, w)


def _merge_kernel(oa_ref, ob_ref, oc_ref, od_ref, g_ref, wb_ref, wg_ref, o_ref):
    lat = g_ref[...].astype(MXU_DTYPE)
    acc = None
    for n, o in enumerate((oa_ref, ob_ref, oc_ref, od_ref)):
        y = _dot(o[...], wb_ref[n])
        gate = jax.nn.sigmoid(_dot(lat, wg_ref[n]))
        acc = gate * y if acc is None else acc + gate * y
    o_ref[...] = acc.astype(o_ref.dtype)


def merge_branches(outs, proj, w_branch, w_gate, layer, tm=1024, tn=512):
    t = proj.shape[0]
    w = BRANCH_W
    tm = min(tm, t)
    n_br = len(outs)
    return pl.pallas_call(
        _merge_kernel,
        grid=(t // tm, D_MODEL // tn),
        in_specs=[pl.BlockSpec((tm, w), lambda i, j: (i, 0))] * n_br
        + [pl.BlockSpec((tm, GATE_RANK), lambda i, j: (i, P_G // GATE_RANK)),
           pl.BlockSpec((None, n_br, w, tn), lambda i, j: (layer, 0, 0, j)),
           pl.BlockSpec((None, n_br, GATE_RANK, tn), lambda i, j: (layer, 0, 0, j))],
        out_specs=pl.BlockSpec((tm, tn), lambda i, j: (i, j)),
        out_shape=jax.ShapeDtypeStruct((t, D_MODEL), MXU_DTYPE),
        compiler_params=_cparams("parallel", "parallel"),
        name="merge",
    )(*outs, proj, w_branch, w_gate)


def _permute_w_in(w):
    z = jnp.zeros(w.shape[:-1] + (N_P - 11872,), w.dtype)
    return jnp.concatenate([
        w[..., 0:1024],
        w[..., 1280:2304],
        w[..., 2384:5456],
        w[..., 5456:6480],
        w[..., 6496:8544],
        w[..., 8544:11616],
        w[..., 11616:11872],
        w[..., 1024:1152],
        w[..., 1152:1280],
        w[..., 2304:2368],
        w[..., 2368:2384],
        w[..., 6480:6488],
        w[..., 6488:6496],
        z], axis=-1)


def kernel(x, mix_norm, w_in, a_q_norm, a_k_norm, a_idx_k_norm, b_conv, b_a_log, b_dt_bias, b_out_norm,
           c_ln_gain, c_ln_bias, c_spatial_w, c_spatial_b, d_conv, w_branch, w_gate_up, w_out, ffn_norm,
           w_ff1, w_ff2):
    bn, seq_len, d = x.shape
    depth = w_in.shape[0]
    bf = MXU_DTYPE
    w_in_p = _permute_w_in(w_in.astype(bf))
    w_branch_b = w_branch.astype(bf).reshape(depth, 4, BRANCH_W, d)
    w_gate_b = w_gate_up.astype(bf)
    w_out_b = w_out.astype(bf)
    w_ff1_b = w_ff1.astype(bf)
    w_ff2_b = w_ff2.astype(bf)

    xt = x.reshape(bn * seq_len, d)
    for l in range(depth):
        h = rmsnorm_rows(xt, mix_norm[l])
        proj = matmul(h, w_in_p, l)
        kn, vt, ki = dsa_prep(proj, a_k_norm[l], a_idx_k_norm[l])
        o_a = dsa_mixer(proj, kn, vt, ki, a_q_norm[l], bn, seq_len)
        bq, bk, bv, bg, bbeta = deltanet_prep(proj, b_conv[l], b_a_log[l], b_dt_bias[l], seq_len)
        o_b = deltanet_mixer(proj, bq, bk, bv, bg, bbeta, b_out_norm[l], bn, seq_len)
        o_c = spatial_gating_mixer(proj, c_ln_gain[l], c_ln_bias[l], c_spatial_w[l], c_spatial_b[l])
        o_d = short_conv_mixer(proj, d_conv[l], seq_len)
        merged = merge_branches((o_a, o_b, o_c, o_d), proj, w_branch_b, w_gate_b, l)
        xt = matmul(merged, w_out_b, l, epilogue="residual", residual=xt, tn=512)
        f = rmsnorm_rows(xt, ffn_norm[l])
        mid = matmul(f, w_ff1_b, l, epilogue="relu2", out_dtype=bf)
        xt = matmul(mid, w_ff2_b, l, epilogue="residual", residual=xt, tk=2048)
    return xt.reshape(bn, seq_len, d)
```

```python
import functools

import jax
import jax.numpy as jnp
from jax import lax
from jax.experimental import pallas as pl
from jax.experimental.pallas import tpu as pltpu

D_MODEL = 4096
HEAD_DIM = 128
BRANCH_W = 1024
A_HEADS = 8
IDX_HEADS = 16
IDX_DIM = 64
IDX_W_SCALE = (IDX_HEADS * IDX_DIM) ** -0.5
TOPK_MAX = 256
Q_BLOCK = 128
B_HEADS = 8
B_CONV = 4
B_CHUNK = 64
C_CHUNK = 128
C_GROUPS = 8
D_CONV = 3
GATE_RANK = 256
EPS = 1e-6

P_AQ, P_IQ, P_BQ, P_BZ, P_CU, P_CV, P_DH, P_DB, P_DC = 0, 1024, 2048, 5120, 6144, 7168, 8192, 9216, 10240
P_G, P_AK, P_AV, P_SM = 11264, 11520, 11648, 11776
N_P = 11904
SM_IK, SM_IW, SM_BA, SM_BB = 0, 64, 80, 88

MXU_DTYPE = jnp.bfloat16
LANES = 128
HALO = 8
VMEM_LIMIT = 56 * 1024 * 1024
NEG_BIG = -1e30
INT_MIN = -(2 ** 31)
KEY_NEG_INF = INT_MIN + 0x7FFFFF
HIGHEST = lax.Precision.HIGHEST


def _cparams(*sem):
    return pltpu.CompilerParams(dimension_semantics=sem, vmem_limit_bytes=VMEM_LIMIT)


def _nt_dot(a, b, precision=None):
    return lax.dot_general(a, b, (((1,), (1,)), ((), ())), precision=precision,
                           preferred_element_type=jnp.float32)


def _dot(a, b, precision=None):
    return jnp.dot(a, b, precision=precision, preferred_element_type=jnp.float32)


def _dot3(a, b):
    a_hi, b_hi = a.astype(MXU_DTYPE), b.astype(MXU_DTYPE)
    a_lo = (a - a_hi.astype(jnp.float32)).astype(MXU_DTYPE)
    b_lo = (b - b_hi.astype(jnp.float32)).astype(MXU_DTYPE)
    return _dot(a_hi, b_hi) + (_dot(a_hi, b_lo) + _dot(a_lo, b_hi))


def _eye(n, dtype):
    r = lax.broadcasted_iota(jnp.int32, (n, n), 0)
    c = lax.broadcasted_iota(jnp.int32, (n, n), 1)
    return (r == c).astype(dtype)


def _lane_partial_sq(x):
    sq = x * x
    out = sq[:, 0:LANES]
    for g in range(1, x.shape[1] // LANES):
        out = out + sq[:, g * LANES:(g + 1) * LANES]
    return out


def _norm_stats_kernel(x_ref, g_ref, xb_ref, ssq_ref):
    x = x_ref[...]
    xb_ref[...] = (x * g_ref[...]).astype(xb_ref.dtype)
    ssq_ref[...] = _lane_partial_sq(x)


def norm_stats(x, gain, tl=256):
    t, d = x.shape
    return pl.pallas_call(
        _norm_stats_kernel,
        grid=(t // tl,),
        in_specs=[pl.BlockSpec((tl, d), lambda i: (i, 0)),
                  pl.BlockSpec((1, d), lambda i: (0, 0))],
        out_specs=[pl.BlockSpec((tl, d), lambda i: (i, 0)),
                   pl.BlockSpec((tl, LANES), lambda i: (i, 0))],
        out_shape=[jax.ShapeDtypeStruct((t, d), MXU_DTYPE),
                   jax.ShapeDtypeStruct((t, LANES), jnp.float32)],
        compiler_params=_cparams("parallel"),
        name="norm_stats",
    )(x, gain.reshape(1, d))


def _mm_kernel(*refs, nk, kdim, epilogue, row_scaled, emit_stats):
    refs = list(refs)
    a_ref, w_ref = refs.pop(0), refs.pop(0)
    ssq_in_ref = refs.pop(0) if row_scaled else None
    r_ref = refs.pop(0) if epilogue == "residual" else None
    g_next_ref = refs.pop(0) if emit_stats else None
    o_ref = refs.pop(0)
    xb_ref, ssq_out_ref = (refs.pop(0), refs.pop(0)) if emit_stats else (None, None)
    scratch = refs
    part = _dot(a_ref[...], w_ref[...])

    def finish(acc):
        if row_scaled:
            acc = acc * lax.rsqrt(jnp.sum(ssq_in_ref[...], axis=-1, keepdims=True) * (1.0 / kdim) + EPS)
        if epilogue == "residual":
            out = r_ref[...] + acc
            o_ref[...] = out
            if emit_stats:
                xb_ref[...] = (out * g_next_ref[...]).astype(xb_ref.dtype)
                j = pl.program_id(1)

                @pl.when(j == 0)
                def _():
                    ssq_out_ref[...] = _lane_partial_sq(out)

                @pl.when(j > 0)
                def _():
                    ssq_out_ref[...] += _lane_partial_sq(out)
        elif epilogue == "relu2":
            o_ref[...] = jnp.square(jnp.maximum(acc, 0.0)).astype(o_ref.dtype)
        else:
            o_ref[...] = acc.astype(o_ref.dtype)

    if nk == 1:
        finish(part)
    else:
        acc_ref = scratch[0]
        k = pl.program_id(2)

        @pl.when(k == 0)
        def _():
            acc_ref[...] = part

        @pl.when(k > 0)
        def _():
            acc_ref[...] += part

        @pl.when(k == nk - 1)
        def _():
            finish(acc_ref[...])


def matmul(a, w_stack, layer, *, epilogue="none", residual=None, row_ssq=None, next_gain=None,
           out_dtype=jnp.float32, tm=1024, tn=1024, tk=4096):
    emit_stats = next_gain is not None
    m, kdim = a.shape
    n = w_stack.shape[2]
    tm, tn, tk = min(tm, m), min(tn, n), min(tk, kdim)
    nk = kdim // tk
    tile = pl.BlockSpec((tm, tn), lambda i, j, k: (i, j))
    stats = pl.BlockSpec((tm, LANES), lambda i, j, k: (i, 0))
    in_specs = [pl.BlockSpec((tm, tk), lambda i, j, k: (i, k)),
                pl.BlockSpec((None, tk, tn), lambda i, j, k: (layer, k, j))]
    args = [a, w_stack]
    if row_ssq is not None:
        in_specs.append(stats)
        args.append(row_ssq)
    if epilogue == "residual":
        in_specs.append(tile)
        args.append(residual)
    if emit_stats:
        in_specs.append(pl.BlockSpec((1, tn), lambda i, j, k: (0, j)))
        args.append(next_gain.reshape(1, n))
    out_specs, out_shape = [tile], [jax.ShapeDtypeStruct((m, n), out_dtype)]
    if emit_stats:
        out_specs += [tile, stats]
        out_shape += [jax.ShapeDtypeStruct((m, n), MXU_DTYPE), jax.ShapeDtypeStruct((m, LANES), jnp.float32)]
    scratch = [pltpu.VMEM((tm, tn), jnp.float32)] if nk > 1 else []
    out = pl.pallas_call(
        functools.partial(_mm_kernel, nk=nk, kdim=kdim, epilogue=epilogue, row_scaled=row_ssq is not None,
                          emit_stats=emit_stats),
        grid=(m // tm, pl.cdiv(n, tn), nk),
        in_specs=in_specs,
        out_specs=out_specs,
        out_shape=out_shape,
        scratch_shapes=scratch,
        compiler_params=_cparams("parallel", "arbitrary", "arbitrary"),
        name="matmul_" + epilogue,
    )(*args)
    return out if emit_stats else out[0]


def _dconv_kernel(h_ref, b_ref, c_ref, hp_ref, cp_ref, w_ref, o_ref, ext_ref, *, tiles_per_seq):
    tl = h_ref.shape[0]
    first = (pl.program_id(0) % tiles_per_seq) == 0
    ext_ref[0:HALO, :] = jnp.where(first, 0.0, cp_ref[...] * hp_ref[...])
    p = c_ref[...] * h_ref[...]
    ext_ref[HALO:HALO + tl, :] = p
    w = w_ref[...]
    acc = w[2:3, :] * p
    acc = acc + w[1:2, :] * ext_ref[HALO - 1:HALO - 1 + tl, :]
    acc = acc + w[0:1, :] * ext_ref[HALO - 2:HALO - 2 + tl, :]
    o_ref[...] = (b_ref[...] * acc).astype(o_ref.dtype)


def short_conv_mixer(proj, conv_w, seq_len, tl=512):
    t = proj.shape[0]
    w = BRANCH_W
    cur = lambda cb: pl.BlockSpec((tl, w), lambda i: (i, cb))
    prev = lambda cb: pl.BlockSpec((HALO, w), lambda i: (jnp.maximum(i * (tl // HALO) - 1, 0), cb))
    return pl.pallas_call(
        functools.partial(_dconv_kernel, tiles_per_seq=seq_len // tl),
        grid=(t // tl,),
        in_specs=[cur(P_DH // w), cur(P_DB // w), cur(P_DC // w), prev(P_DH // w), prev(P_DC // w),
                  pl.BlockSpec((D_CONV, w), lambda i: (0, 0))],
        out_specs=pl.BlockSpec((tl, w), lambda i: (i, 0)),
        out_shape=jax.ShapeDtypeStruct((t, w), MXU_DTYPE),
        scratch_shapes=[pltpu.VMEM((tl + HALO, w), jnp.float32)],
        compiler_params=_cparams("parallel"),
        name="mixer_d",
    )(proj, proj, proj, proj, proj, conv_w)


def _gmlp_kernel(u_ref, v_ref, lg_ref, lb_ref, ws_ref, bs_ref, o_ref):
    tl = u_ref.shape[0]
    gd = BRANCH_W // C_GROUPS
    row = lax.broadcasted_iota(jnp.int32, (C_CHUNK, C_CHUNK), 0)
    col = lax.broadcasted_iota(jnp.int32, (C_CHUNK, C_CHUNK), 1)
    causal = row >= col
    for g in range(C_GROUPS):
        w_m = jnp.where(causal, ws_ref[g], 0.0).astype(MXU_DTYPE)
        bias = bs_ref[:, g:g + 1]
        gain = lg_ref[:, g * gd:(g + 1) * gd]
        shift = lb_ref[:, g * gd:(g + 1) * gd]
        for c in range(tl // C_CHUNK):
            rows = slice(c * C_CHUNK, (c + 1) * C_CHUNK)
            cols = slice(g * gd, (g + 1) * gd)
            v = jax.nn.gelu(v_ref[rows, cols])
            mu = jnp.mean(v, axis=-1, keepdims=True)
            vc = v - mu
            var = jnp.mean(vc * vc, axis=-1, keepdims=True)
            vn = vc * lax.rsqrt(var + EPS) * gain + shift
            s = _dot(w_m, vn.astype(MXU_DTYPE)) + bias
            o_ref[rows, cols] = (jax.nn.gelu(u_ref[rows, cols]) * s).astype(o_ref.dtype)


def spatial_gating_mixer(proj, ln_gain, ln_bias, w_s, b_s, tl=256):
    t = proj.shape[0]
    w = BRANCH_W
    return pl.pallas_call(
        _gmlp_kernel,
        grid=(t // tl,),
        in_specs=[pl.BlockSpec((tl, w), lambda i: (i, P_CU // w)),
                  pl.BlockSpec((tl, w), lambda i: (i, P_CV // w)),
                  pl.BlockSpec((1, w), lambda i: (0, 0)),
                  pl.BlockSpec((1, w), lambda i: (0, 0)),
                  pl.BlockSpec((C_GROUPS, C_CHUNK, C_CHUNK), lambda i: (0, 0, 0)),
                  pl.BlockSpec((C_CHUNK, C_GROUPS), lambda i: (0, 0))],
        out_specs=pl.BlockSpec((tl, w), lambda i: (i, 0)),
        out_shape=jax.ShapeDtypeStruct((t, w), MXU_DTYPE),
        compiler_params=_cparams("parallel"),
        name="mixer_c",
    )(proj, proj, ln_gain.reshape(1, w), ln_bias.reshape(1, w), w_s, b_s.T)


def _a_prep_kernel(k_ref, v_ref, sm_ref, gk_ref, gi_ref, kn_ref, vt_ref, ki_ref):
    k = k_ref[...]
    kn = k * lax.rsqrt(jnp.mean(k * k, axis=-1, keepdims=True) + EPS) * gk_ref[...]
    kn_ref[...] = kn.astype(kn_ref.dtype)
    vt_ref[...] = _nt_dot(_eye(HEAD_DIM, MXU_DTYPE), v_ref[...].astype(MXU_DTYPE)).astype(vt_ref.dtype)
    ik = sm_ref[:, SM_IK:SM_IK + IDX_DIM]
    ikn = ik * lax.rsqrt(jnp.mean(ik * ik, axis=-1, keepdims=True) + EPS) * gi_ref[...]
    ki_ref[...] = ikn.astype(ki_ref.dtype)


def dsa_prep(proj, k_gain, ik_gain, tl=512):
    t = proj.shape[0]
    hd = HEAD_DIM
    return pl.pallas_call(
        _a_prep_kernel,
        grid=(t // tl,),
        in_specs=[pl.BlockSpec((tl, hd), lambda i: (i, P_AK // hd)),
                  pl.BlockSpec((tl, hd), lambda i: (i, P_AV // hd)),
                  pl.BlockSpec((tl, hd), lambda i: (i, P_SM // hd)),
                  pl.BlockSpec((1, hd), lambda i: (0, 0)),
                  pl.BlockSpec((1, IDX_DIM), lambda i: (0, 0))],
        out_specs=[pl.BlockSpec((tl, hd), lambda i: (i, 0)),
                   pl.BlockSpec((hd, tl), lambda i: (0, i)),
                   pl.BlockSpec((tl, IDX_DIM), lambda i: (i, 0))],
        out_shape=[jax.ShapeDtypeStruct((t, hd), MXU_DTYPE),
                   jax.ShapeDtypeStruct((hd, t), MXU_DTYPE),
                   jax.ShapeDtypeStruct((t, IDX_DIM), MXU_DTYPE)],
        compiler_params=_cparams("parallel"),
        name="mixer_a_prep",
    )(proj, proj, proj, k_gain.reshape(1, hd), ik_gain.reshape(1, IDX_DIM))


def _dsa_kernel(q_ref, iq_ref, sm_ref, kn_ref, vt_ref, ki_ref, gq_ref, o_ref,
                keys_ref, qt_ref, qit_ref, m_ref, l_ref, acc_ref, *, topk, ck, seq_len):
    qb = Q_BLOCK
    bf = MXU_DTYPE
    f32 = jnp.float32
    i = pl.program_id(1)
    n_chunks = (i * qb + qb + ck - 1) // ck

    def chunk_start(c):
        return pl.multiple_of(c * ck, ck)

    def col_sum(x):
        return jnp.sum(jnp.sum(x.reshape(4, ck // 32, 8, qb), axis=1), axis=0)

    gq = gq_ref[...]
    eye_d = _eye(HEAD_DIM, bf)
    for h in range(A_HEADS):
        qh = q_ref[:, h * HEAD_DIM:(h + 1) * HEAD_DIM]
        qn = qh * lax.rsqrt(jnp.mean(qh * qh, axis=-1, keepdims=True) + EPS) * gq
        qt_ref[:, h * qb:(h + 1) * qb] = _nt_dot(eye_d, qn.astype(bf)).astype(bf)
    eye_i = _eye(IDX_DIM, bf)
    for h in range(IDX_HEADS):
        qi = iq_ref[:, h * IDX_DIM:(h + 1) * IDX_DIM].astype(bf)
        qit_ref[:, h * qb:(h + 1) * qb] = _nt_dot(eye_i, qi).astype(bf)
    w_t = _nt_dot(_eye(IDX_HEADS, f32), sm_ref[:, SM_IW:SM_IW + IDX_HEADS] * IDX_W_SCALE,
                  precision=HIGHEST)

    q_pos = i * qb + lax.broadcasted_iota(jnp.int32, (ck, qb), 1)
    k_off = lax.broadcasted_iota(jnp.int32, (ck, qb), 0)

    def score_chunk(c, carry):
        start = chunk_start(c)
        kc = ki_ref[pl.ds(start, ck), :]
        acc = jnp.zeros((ck, qb), f32)
        for hp in range(IDX_HEADS // 2):
            r = _dot(kc, qit_ref[:, hp * 2 * qb:(hp + 1) * 2 * qb])
            for s in range(2):
                h = 2 * hp + s
                acc = acc + w_t[h:h + 1, :] * jnp.maximum(r[:, s * qb:(s + 1) * qb], 0.0)
        score = jnp.where(k_off + start <= q_pos, acc, -jnp.inf)
        bits = pltpu.bitcast(score, jnp.int32)
        keys_ref[pl.ds(start, ck), :] = bits ^ ((bits >> 31) & 0x7FFFFFFF)
        return carry

    lax.fori_loop(0, n_chunks, score_chunk, 0)

    def count(pred):
        def hits(c):
            start = chunk_start(c)
            return col_sum(jnp.where(pred(keys_ref[pl.ds(start, ck), :], k_off + start), 1.0, 0.0))

        def two_chunks(j, cnt):
            second = jnp.minimum(2 * j + 1, n_chunks - 1)
            valid = (2 * j + 1 < n_chunks).astype(f32)
            return cnt + hits(2 * j) + hits(second) * valid

        cnt = lax.fori_loop(0, (n_chunks + 1) // 2, two_chunks, jnp.zeros((8, qb), f32))
        return jnp.sum(cnt, axis=0, keepdims=True)

    def bit_step(b, tau_u):
        cand_u = tau_u | lax.shift_left(jnp.int32(1), 31 - b)
        cand_s = cand_u ^ INT_MIN
        total = count(lambda kk, _: kk >= cand_s)
        return jnp.where(total >= float(topk), cand_u, tau_u)

    tau_u = lax.fori_loop(0, 32, bit_step, jnp.zeros((1, qb), jnp.int32))
    tau = jnp.maximum(tau_u ^ INT_MIN, KEY_NEG_INF + 1)

    need = float(topk) - count(lambda kk, _: kk > tau)
    n_equal = count(lambda kk, _: kk == tau)
    has_tie = jnp.max(jnp.where(n_equal > need, 1.0, 0.0)) > 0.5

    @pl.when(has_tie)
    def _():
        nbits = seq_len.bit_length()

        def tie_bit(b, x):
            cand = x | lax.shift_left(jnp.int32(1), nbits - 1 - b)
            below = count(lambda kk, kidx: jnp.where(kk == tau, kidx, cand) < cand)
            return jnp.where(below < need, cand, x)

        x = lax.fori_loop(0, nbits, tie_bit, jnp.zeros((1, qb), jnp.int32))

        def demote(c, carry):
            start = chunk_start(c)
            kk = keys_ref[pl.ds(start, ck), :]
            drop = jnp.where(kk == tau, k_off + start, x) > x
            keys_ref[pl.ds(start, ck), :] = jnp.where(drop, KEY_NEG_INF, kk)
            return carry

        lax.fori_loop(0, n_chunks, demote, 0)

    m_ref[...] = jnp.full(m_ref.shape, NEG_BIG, f32)
    l_ref[...] = jnp.zeros(l_ref.shape, f32)
    acc_ref[...] = jnp.zeros(acc_ref.shape, f32)
    c_exp = (HEAD_DIM ** -0.5) * 1.4426950408889634
    pairs = [slice(hp * 2 * qb, (hp + 1) * 2 * qb) for hp in range(A_HEADS // 2)]

    def attn_chunk(c, carry):
        start = chunk_start(c)
        kc = kn_ref[pl.ds(start, ck), :]
        vt = vt_ref[:, pl.ds(start, ck)]
        sel = keys_ref[pl.ds(start, ck), :] >= tau
        raw = [_dot(kc, qt_ref[:, cols]) for cols in pairs]
        probs, alphas = [], []
        for cols, s in zip(pairs, raw):
            s = jnp.concatenate([jnp.where(sel, s[:, :qb], NEG_BIG), jnp.where(sel, s[:, qb:], NEG_BIG)], axis=1)
            m_old = m_ref[:, cols]
            m_new = jnp.maximum(m_old, jnp.max(s, axis=0, keepdims=True))
            alpha = jnp.exp2((m_old - m_new) * c_exp)
            p = jnp.exp2((s - m_new) * c_exp)
            l_ref[:, cols] = alpha * l_ref[:, cols] + jnp.sum(p, axis=0, keepdims=True)
            m_ref[:, cols] = m_new
            probs.append(p.astype(bf))
            alphas.append(alpha)
        for cols, p, alpha in zip(pairs, probs, alphas):
            acc_ref[:, cols] = alpha * acc_ref[:, cols] + _dot(vt, p)
        return carry

    lax.fori_loop(0, n_chunks, attn_chunk, 0)

    for h in range(A_HEADS):
        cols = slice(h * qb, (h + 1) * qb)
        o_t = acc_ref[:, cols] / l_ref[:, cols]
        o_ref[:, h * HEAD_DIM:(h + 1) * HEAD_DIM] = o_t.T.astype(o_ref.dtype)


def dsa_mixer(proj, kn, vt, ki, q_gain, batch, seq_len, ck=512):
    t = proj.shape[0]
    qb = Q_BLOCK
    nq = seq_len // qb
    ck = min(ck, seq_len)
    topk = min(TOPK_MAX, seq_len // 4)
    w = BRANCH_W
    return pl.pallas_call(
        functools.partial(_dsa_kernel, topk=topk, ck=ck, seq_len=seq_len),
        grid=(batch, nq),
        in_specs=[pl.BlockSpec((qb, w), lambda b, i: (b * nq + i, P_AQ // w)),
                  pl.BlockSpec((qb, w), lambda b, i: (b * nq + i, P_IQ // w)),
                  pl.BlockSpec((qb, HEAD_DIM), lambda b, i: (b * nq + i, P_SM // HEAD_DIM)),
                  pl.BlockSpec((seq_len, HEAD_DIM), lambda b, i: (b, 0)),
                  pl.BlockSpec((HEAD_DIM, seq_len), lambda b, i: (0, b)),
                  pl.BlockSpec((seq_len, IDX_DIM), lambda b, i: (b, 0)),
                  pl.BlockSpec((1, HEAD_DIM), lambda b, i: (0, 0))],
        out_specs=pl.BlockSpec((qb, w), lambda b, i: (b * nq + i, 0)),
        out_shape=jax.ShapeDtypeStruct((t, w), MXU_DTYPE),
        scratch_shapes=[pltpu.VMEM((seq_len, qb), jnp.int32),
                        pltpu.VMEM((HEAD_DIM, A_HEADS * qb), MXU_DTYPE),
                        pltpu.VMEM((IDX_DIM, IDX_HEADS * qb), MXU_DTYPE),
                        pltpu.VMEM((1, A_HEADS * qb), jnp.float32),
                        pltpu.VMEM((1, A_HEADS * qb), jnp.float32),
                        pltpu.VMEM((HEAD_DIM, A_HEADS * qb), jnp.float32)],
        compiler_params=_cparams("parallel", "arbitrary"),
        name="mixer_a",
    )(proj, proj, proj, kn, vt, ki, q_gain.reshape(1, HEAD_DIM))


def _b_prep_kernel(q_ref, k_ref, v_ref, qp_ref, kp_ref, vp_ref, sm_ref, cw_ref, alog_ref, dtb_ref,
                   qo_ref, ko_ref, vo_ref, g_ref, beta_ref, ext_ref, *, tiles_per_seq):
    tl = q_ref.shape[0]
    w = BRANCH_W
    first = (pl.program_id(0) % tiles_per_seq) == 0
    parts = ((q_ref, qp_ref, qo_ref, True), (k_ref, kp_ref, ko_ref, True), (v_ref, vp_ref, vo_ref, False))
    for n, (cur, prev, out, normed) in enumerate(parts):
        ext_ref[0:HALO, :] = jnp.where(first, 0.0, prev[...])
        ext_ref[HALO:HALO + tl, :] = cur[...]
        cw = cw_ref[:, n * w:(n + 1) * w]
        acc = cw[B_CONV - 1:B_CONV, :] * cur[...]
        for j in range(B_CONV - 1):
            off = HALO - (B_CONV - 1) + j
            acc = acc + cw[j:j + 1, :] * ext_ref[off:off + tl, :]
        y = acc * jax.nn.sigmoid(acc)
        if normed:
            for h in range(B_HEADS):
                cols = slice(h * HEAD_DIM, (h + 1) * HEAD_DIM)
                yh = y[:, cols]
                out[:, cols] = yh * lax.rsqrt(jnp.sum(yh * yh, axis=-1, keepdims=True) + EPS)
        else:
            out[...] = y
    a = sm_ref[:, SM_BA:SM_BA + B_HEADS] + dtb_ref[...]
    softplus = jnp.maximum(a, 0.0) + jnp.log(1.0 + jnp.exp(-jnp.abs(a)))
    g_ref[...] = -jnp.exp(alog_ref[...]) * softplus
    beta_ref[...] = jax.nn.sigmoid(sm_ref[:, SM_BB:SM_BB + B_HEADS])


def deltanet_prep(proj, conv_w, a_log, dt_bias, seq_len, tl=256):
    t = proj.shape[0]
    w = BRANCH_W
    cb = P_BQ // w
    cur = lambda c: pl.BlockSpec((tl, w), lambda i: (i, c))
    prev = lambda c: pl.BlockSpec((HALO, w), lambda i: (jnp.maximum(i * (tl // HALO) - 1, 0), c))
    row = lambda width: pl.BlockSpec((tl, width), lambda i: (i, 0))
    return pl.pallas_call(
        functools.partial(_b_prep_kernel, tiles_per_seq=seq_len // tl),
        grid=(t // tl,),
        in_specs=[cur(cb), cur(cb + 1), cur(cb + 2), prev(cb), prev(cb + 1), prev(cb + 2),
                  pl.BlockSpec((tl, HEAD_DIM), lambda i: (i, P_SM // HEAD_DIM)),
                  pl.BlockSpec((B_CONV, 3 * w), lambda i: (0, 0)),
                  pl.BlockSpec((1, B_HEADS), lambda i: (0, 0)),
                  pl.BlockSpec((1, B_HEADS), lambda i: (0, 0))],
        out_specs=[row(w), row(w), row(w), row(B_HEADS), row(B_HEADS)],
        out_shape=[jax.ShapeDtypeStruct((t, w), jnp.float32)] * 3
        + [jax.ShapeDtypeStruct((t, B_HEADS), jnp.float32)] * 2,
        scratch_shapes=[pltpu.VMEM((tl + HALO, w), jnp.float32)],
        compiler_params=_cparams("parallel"),
        name="mixer_b_prep",
    )(proj, proj, proj, proj, proj, proj, proj, conv_w, a_log.reshape(1, B_HEADS), dt_bias.reshape(1, B_HEADS))


def _delta_kernel(q_ref, k_ref, v_ref, g_ref, beta_ref, z_ref, gain_ref, o_ref, state_ref):
    nb = q_ref.shape[0]
    cs = B_CHUNK
    bf = MXU_DTYPE
    f32 = jnp.float32

    @pl.when(pl.program_id(0) == 0)
    def _():
        state_ref[...] = jnp.zeros(state_ref.shape, f32)

    row = lax.broadcasted_iota(jnp.int32, (cs, cs), 0)
    col = lax.broadcasted_iota(jnp.int32, (cs, cs), 1)
    incl = row >= col
    strict = row > col
    eye_c = (row == col).astype(f32)
    eye_h = _eye(B_HEADS, f32)
    eye_d = _eye(HEAD_DIM, bf)
    tril = incl.astype(f32)
    gain = gain_ref[...]
    qscale = HEAD_DIM ** -0.5

    chains = []
    for b in range(nb):
        gc = _dot(tril, g_ref[b], precision=HIGHEST)
        gc_t = _nt_dot(eye_h, gc, precision=HIGHEST)
        eg = jnp.exp(gc)
        g_last = gc[cs - 1:cs, :]
        ek = jnp.exp(g_last - gc)
        e_last = jnp.exp(g_last)
        beta = beta_ref[b]
        for h in range(B_HEADS):
            cols = slice(h * HEAD_DIM, (h + 1) * HEAD_DIM)
            qh = q_ref[b, :, cols] * qscale
            kh = k_ref[b, :, cols]
            bh = beta[:, h:h + 1]
            kb = kh * bh
            decay = jnp.exp(jnp.where(incl, gc[:, h:h + 1] - gc_t[h:h + 1, :], -jnp.inf))
            chains.append(dict(
                b=b, h=h, cols=cols, idx=b * B_HEADS + h, decay=decay,
                q_b=qh.astype(bf), k_b=kh.astype(bf), kb_b=kb.astype(bf),
                vb_b=(v_ref[b, :, cols] * bh).astype(bf),
                kbe_b=(kb * eg[:, h:h + 1]).astype(bf),
                qdec_b=(qh * eg[:, h:h + 1]).astype(bf),
                kdec_b=(kh * ek[:, h:h + 1]).astype(bf),
                e_last=e_last[:, h:h + 1]))

    for ch in chains:
        ch["lower"] = jnp.where(strict, _nt_dot(ch["kb_b"], ch["k_b"]) * ch["decay"], 0.0)
    for ch in chains:
        ch["attn_b"] = jnp.where(incl, _nt_dot(ch["q_b"], ch["k_b"]) * ch["decay"], 0.0).astype(bf)
    for ch in chains:
        ch["inv"] = eye_c - ch["lower"]
        ch["pw"] = ch["lower"]
    for _ in range(5):
        for ch in chains:
            ch["pw"] = _dot3(ch["pw"], ch["pw"])
        for ch in chains:
            ch["inv"] = ch["inv"] + _dot3(ch["inv"], ch["pw"])
    for ch in chains:
        ch["inv_b"] = ch["inv"].astype(bf)
        ch["u"] = _dot(ch["inv_b"], ch["vb_b"])
    for ch in chains:
        ch["w_b"] = _dot(ch["inv_b"], ch["kbe_b"]).astype(bf)
    for ch in chains:
        ch["state_b"] = state_ref[ch["idx"]].astype(bf)
        ch["vnew_b"] = (ch["u"] - _dot(ch["w_b"], ch["state_b"])).astype(bf)
    for ch in chains:
        ch["out"] = _dot(ch["qdec_b"], ch["state_b"]) + _dot(ch["attn_b"], ch["vnew_b"])
    for ch in chains:
        ch["kdec_t"] = _nt_dot(eye_d, ch["kdec_b"]).astype(bf)
    for ch in chains:
        state_ref[ch["idx"]] = state_ref[ch["idx"]] * ch["e_last"] + _dot(ch["kdec_t"], ch["vnew_b"])
    for ch in chains:
        out = ch["out"]
        on = out * lax.rsqrt(jnp.mean(out * out, axis=-1, keepdims=True) + EPS) * gain
        z = z_ref[ch["b"], :, ch["cols"]]
        o_ref[ch["b"], :, ch["cols"]] = (on * (z * jax.nn.sigmoid(z))).astype(o_ref.dtype)


def deltanet_mixer(proj, q, k, v, g, beta, out_gain, batch, seq_len):
    w = BRANCH_W
    cs = B_CHUNK
    r3 = lambda a: a.reshape(batch, seq_len, a.shape[-1])
    blk = lambda width, cb=0: pl.BlockSpec((batch, cs, width), lambda c: (0, c, cb))
    out = pl.pallas_call(
        _delta_kernel,
        grid=(seq_len // cs,),
        in_specs=[blk(w), blk(w), blk(w), blk(B_HEADS), blk(B_HEADS), blk(w, P_BZ // w),
                  pl.BlockSpec((1, HEAD_DIM), lambda c: (0, 0))],
        out_specs=blk(w),
        out_shape=jax.ShapeDtypeStruct((batch, seq_len, w), MXU_DTYPE),
        scratch_shapes=[pltpu.VMEM((batch * B_HEADS, HEAD_DIM, HEAD_DIM), jnp.float32)],
        compiler_params=_cparams("arbitrary"),
        name="mixer_b",
    )(r3(q), r3(k), r3(v), r3(g), r3(beta), r3(proj), out_gain.reshape(1, HEAD_DIM))
    return out.reshape(batch * seq_len, w)


def _merge_kernel(oa_ref, ob_ref, oc_ref, od_ref, g_ref, wb_ref, wg_ref, o_ref):
    lat = g_ref[...].astype(MXU_DTYPE)
    acc = None
    for n, o in enumerate((oa_ref, ob_ref, oc_ref, od_ref)):
        y = _dot(o[...], wb_ref[n])
        gate = jax.nn.sigmoid(_dot(lat, wg_ref[n]))
        acc = gate * y if acc is None else acc + gate * y
    o_ref[...] = acc.astype(o_ref.dtype)


def merge_branches(outs, proj, w_branch, w_gate, layer, tm=1024, tn=512):
    t = proj.shape[0]
    w = BRANCH_W
    tm = min(tm, t)
    n_br = len(outs)
    return pl.pallas_call(
        _merge_kernel,
        grid=(t // tm, D_MODEL // tn),
        in_specs=[pl.BlockSpec((tm, w), lambda i, j: (i, 0))] * n_br
        + [pl.BlockSpec((tm, GATE_RANK), lambda i, j: (i, P_G // GATE_RANK)),
           pl.BlockSpec((None, n_br, w, tn), lambda i, j: (layer, 0, 0, j)),
           pl.BlockSpec((None, n_br, GATE_RANK, tn), lambda i, j: (layer, 0, 0, j))],
        out_specs=pl.BlockSpec((tm, tn), lambda i, j: (i, j)),
        out_shape=jax.ShapeDtypeStruct((t, D_MODEL), MXU_DTYPE),
        compiler_params=_cparams("parallel", "parallel"),
        name="merge",
    )(*outs, proj, w_branch, w_gate)


def _permute_w_in(w):
    z = jnp.zeros(w.shape[:-1] + (N_P - 11872,), w.dtype)
    return jnp.concatenate([
        w[..., 0:1024],
        w[..., 1280:2304],
        w[..., 2384:5456],
        w[..., 5456:6480],
        w[..., 6496:8544],
        w[..., 8544:11616],
        w[..., 11616:11872],
        w[..., 1024:1152],
        w[..., 1152:1280],
        w[..., 2304:2368],
        w[..., 2368:2384],
        w[..., 6480:6488],
        w[..., 6488:6496],
        z], axis=-1)


def kernel(x, mix_norm, w_in, a_q_norm, a_k_norm, a_idx_k_norm, b_conv, b_a_log, b_dt_bias, b_out_norm,
           c_ln_gain, c_ln_bias, c_spatial_w, c_spatial_b, d_conv, w_branch, w_gate_up, w_out, ffn_norm,
           w_ff1, w_ff2):
    bn, seq_len, d = x.shape
    depth = w_in.shape[0]
    bf = MXU_DTYPE
    w_in_p = _permute_w_in(w_in.astype(bf))
    w_branch_b = w_branch.astype(bf).reshape(depth, 4, BRANCH_W, d)
    w_gate_b = w_gate_up.astype(bf)
    w_out_b = w_out.astype(bf)
    w_ff1_b = w_ff1.astype(bf)
    w_ff2_b = w_ff2.astype(bf)

    xt = x.reshape(bn * seq_len, d)
    xb, ssq = norm_stats(xt, mix_norm[0])
    for l in range(depth):
        proj = matmul(xb, w_in_p, l, row_ssq=ssq)
        kn, vt, ki = dsa_prep(proj, a_k_norm[l], a_idx_k_norm[l])
        o_a = dsa_mixer(proj, kn, vt, ki, a_q_norm[l], bn, seq_len)
        bq, bk, bv, bg, bbeta = deltanet_prep(proj, b_conv[l], b_a_log[l], b_dt_bias[l], seq_len)
        o_b = deltanet_mixer(proj, bq, bk, bv, bg, bbeta, b_out_norm[l], bn, seq_len)
        o_c = spatial_gating_mixer(proj, c_ln_gain[l], c_ln_bias[l], c_spatial_w[l], c_spatial_b[l])
        o_d = short_conv_mixer(proj, d_conv[l], seq_len)
        merged = merge_branches((o_a, o_b, o_c, o_d), proj, w_branch_b, w_gate_b, l)
        xt, xb, ssq = matmul(merged, w_out_b, l, epilogue="residual", residual=xt, next_gain=ffn_norm[l], tn=512)
        mid = matmul(xb, w_ff1_b, l, epilogue="relu2", row_ssq=ssq, out_dtype=bf)
        ff2_tiles = dict(tm=512, tn=256, tk=w_ff2_b.shape[1])
        if l + 1 < depth:
            xt, xb, ssq = matmul(mid, w_ff2_b, l, epilogue="residual", residual=xt, next_gain=mix_norm[l + 1],
                                 **ff2_tiles)
        else:
            xt = matmul(mid, w_ff2_b, l, epilogue="residual", residual=xt, **ff2_tiles)
    return xt.reshape(bn, seq_len, d)
```

```python
import functools

import jax
import jax.numpy as jnp
from jax import lax
from jax.experimental import pallas as pl
from jax.experimental.pallas import tpu as pltpu

D_MODEL = 4096
HEAD_DIM = 128
BRANCH_W = 1024
A_HEADS = 8
IDX_HEADS = 16
IDX_DIM = 64
IDX_W_SCALE = (IDX_HEADS * IDX_DIM) ** -0.5
TOPK_MAX = 256
Q_BLOCK = 128
B_HEADS = 8
B_CONV = 4
B_CHUNK = 64
C_CHUNK = 128
C_GROUPS = 8
D_CONV = 3
GATE_RANK = 256
EPS = 1e-6

P_AQ, P_IQ, P_BQ, P_BZ, P_CU, P_CV, P_DH, P_DB, P_DC = 0, 1024, 2048, 5120, 6144, 7168, 8192, 9216, 10240
P_G, P_AK, P_AV, P_SM = 11264, 11520, 11648, 11776
N_P = 11904
SM_IK, SM_IW, SM_BA, SM_BB = 0, 64, 80, 88

MXU_DTYPE = jnp.bfloat16
LANES = 128
HALO = 8
VMEM_LIMIT = 56 * 1024 * 1024
NEG_BIG = -1e30
INT_MIN = -(2 ** 31)
KEY_NEG_INF = INT_MIN + 0x7FFFFF
HIGHEST = lax.Precision.HIGHEST


def _cparams(*sem):
    return pltpu.CompilerParams(dimension_semantics=sem, vmem_limit_bytes=VMEM_LIMIT)


def _nt_dot(a, b, precision=None):
    return lax.dot_general(a, b, (((1,), (1,)), ((), ())), precision=precision,
                           preferred_element_type=jnp.float32)


def _dot(a, b, precision=None):
    return jnp.dot(a, b, precision=precision, preferred_element_type=jnp.float32)


def _dot3(a, b):
    a_hi, b_hi = a.astype(MXU_DTYPE), b.astype(MXU_DTYPE)
    a_lo = (a - a_hi.astype(jnp.float32)).astype(MXU_DTYPE)
    b_lo = (b - b_hi.astype(jnp.float32)).astype(MXU_DTYPE)
    return _dot(a_hi, b_hi) + (_dot(a_hi, b_lo) + _dot(a_lo, b_hi))


def _eye(n, dtype):
    r = lax.broadcasted_iota(jnp.int32, (n, n), 0)
    c = lax.broadcasted_iota(jnp.int32, (n, n), 1)
    return (r == c).astype(dtype)


def _lane_partial_sq(x):
    sq = x * x
    out = sq[:, 0:LANES]
    for g in range(1, x.shape[1] // LANES):
        out = out + sq[:, g * LANES:(g + 1) * LANES]
    return out


def _norm_stats_kernel(x_ref, g_ref, xb_ref, ssq_ref):
    x = x_ref[...]
    xb_ref[...] = (x * g_ref[...]).astype(xb_ref.dtype)
    ssq_ref[...] = _lane_partial_sq(x)


def norm_stats(x, gain, tl=256):
    t, d = x.shape
    return pl.pallas_call(
        _norm_stats_kernel,
        grid=(t // tl,),
        in_specs=[pl.BlockSpec((tl, d), lambda i: (i, 0)),
                  pl.BlockSpec((1, d), lambda i: (0, 0))],
        out_specs=[pl.BlockSpec((tl, d), lambda i: (i, 0)),
                   pl.BlockSpec((tl, LANES), lambda i: (i, 0))],
        out_shape=[jax.ShapeDtypeStruct((t, d), MXU_DTYPE),
                   jax.ShapeDtypeStruct((t, LANES), jnp.float32)],
        compiler_params=_cparams("parallel"),
        name="norm_stats",
    )(x, gain.reshape(1, d))


def _mm_kernel(*refs, nk, kdim, epilogue, row_scaled, emit_stats):
    refs = list(refs)
    a_ref, w_ref = refs.pop(0), refs.pop(0)
    ssq_in_ref = refs.pop(0) if row_scaled else None
    r_ref = refs.pop(0) if epilogue == "residual" else None
    g_next_ref = refs.pop(0) if emit_stats else None
    o_ref = refs.pop(0)
    xb_ref, ssq_out_ref = (refs.pop(0), refs.pop(0)) if emit_stats else (None, None)
    scratch = refs
    part = _dot(a_ref[...], w_ref[...].astype(MXU_DTYPE))

    def finish(acc):
        if row_scaled:
            acc = acc * lax.rsqrt(jnp.sum(ssq_in_ref[...], axis=-1, keepdims=True) * (1.0 / kdim) + EPS)
        if epilogue == "residual":
            out = r_ref[...] + acc
            o_ref[...] = out
            if emit_stats:
                xb_ref[...] = (out * g_next_ref[...]).astype(xb_ref.dtype)
                j = pl.program_id(1)

                @pl.when(j == 0)
                def _():
                    ssq_out_ref[...] = _lane_partial_sq(out)

                @pl.when(j > 0)
                def _():
                    ssq_out_ref[...] += _lane_partial_sq(out)
        elif epilogue == "relu2":
            o_ref[...] = jnp.square(jnp.maximum(acc, 0.0)).astype(o_ref.dtype)
        else:
            o_ref[...] = acc.astype(o_ref.dtype)

    if nk == 1:
        finish(part)
    else:
        acc_ref = scratch[0]
        k = pl.program_id(2)

        @pl.when(k == 0)
        def _():
            acc_ref[...] = part

        @pl.when(k > 0)
        def _():
            acc_ref[...] += part

        @pl.when(k == nk - 1)
        def _():
            finish(acc_ref[...])


def matmul(a, w_stack, layer, *, epilogue="none", residual=None, row_ssq=None, next_gain=None,
           out_dtype=jnp.float32, tm=1024, tn=1024, tk=4096):
    emit_stats = next_gain is not None
    m, kdim = a.shape
    n = w_stack.shape[2]
    tm, tn, tk = min(tm, m), min(tn, n), min(tk, kdim)
    nk = kdim // tk
    tile = pl.BlockSpec((tm, tn), lambda i, j, k: (i, j))
    stats = pl.BlockSpec((tm, LANES), lambda i, j, k: (i, 0))
    in_specs = [pl.BlockSpec((tm, tk), lambda i, j, k: (i, k)),
                pl.BlockSpec((None, tk, tn), lambda i, j, k: (layer, k, j))]
    args = [a, w_stack]
    if row_ssq is not None:
        in_specs.append(stats)
        args.append(row_ssq)
    if epilogue == "residual":
        in_specs.append(tile)
        args.append(residual)
    if emit_stats:
        in_specs.append(pl.BlockSpec((1, tn), lambda i, j, k: (0, j)))
        args.append(next_gain.reshape(1, n))
    out_specs, out_shape = [tile], [jax.ShapeDtypeStruct((m, n), out_dtype)]
    if emit_stats:
        out_specs += [tile, stats]
        out_shape += [jax.ShapeDtypeStruct((m, n), MXU_DTYPE), jax.ShapeDtypeStruct((m, LANES), jnp.float32)]
    scratch = [pltpu.VMEM((tm, tn), jnp.float32)] if nk > 1 else []
    out = pl.pallas_call(
        functools.partial(_mm_kernel, nk=nk, kdim=kdim, epilogue=epilogue, row_scaled=row_ssq is not None,
                          emit_stats=emit_stats),
        grid=(m // tm, pl.cdiv(n, tn), nk),
        in_specs=in_specs,
        out_specs=out_specs,
        out_shape=out_shape,
        scratch_shapes=scratch,
        compiler_params=_cparams("parallel", "arbitrary", "arbitrary"),
        name="matmul_" + epilogue,
    )(*args)
    return out if emit_stats else out[0]


def _dconv_kernel(h_ref, b_ref, c_ref, hp_ref, cp_ref, w_ref, o_ref, ext_ref, *, tiles_per_seq):
    tl = h_ref.shape[0]
    first = (pl.program_id(0) % tiles_per_seq) == 0
    ext_ref[0:HALO, :] = jnp.where(first, 0.0, cp_ref[...] * hp_ref[...])
    p = c_ref[...] * h_ref[...]
    ext_ref[HALO:HALO + tl, :] = p
    w = w_ref[...]
    acc = w[2:3, :] * p
    acc = acc + w[1:2, :] * ext_ref[HALO - 1:HALO - 1 + tl, :]
    acc = acc + w[0:1, :] * ext_ref[HALO - 2:HALO - 2 + tl, :]
    o_ref[...] = (b_ref[...] * acc).astype(o_ref.dtype)


def short_conv_mixer(proj, conv_w, seq_len, tl=512):
    t = proj.shape[0]
    w = BRANCH_W
    cur = lambda cb: pl.BlockSpec((tl, w), lambda i: (i, cb))
    prev = lambda cb: pl.BlockSpec((HALO, w), lambda i: (jnp.maximum(i * (tl // HALO) - 1, 0), cb))
    return pl.pallas_call(
        functools.partial(_dconv_kernel, tiles_per_seq=seq_len // tl),
        grid=(t // tl,),
        in_specs=[cur(P_DH // w), cur(P_DB // w), cur(P_DC // w), prev(P_DH // w), prev(P_DC // w),
                  pl.BlockSpec((D_CONV, w), lambda i: (0, 0))],
        out_specs=pl.BlockSpec((tl, w), lambda i: (i, 0)),
        out_shape=jax.ShapeDtypeStruct((t, w), MXU_DTYPE),
        scratch_shapes=[pltpu.VMEM((tl + HALO, w), jnp.float32)],
        compiler_params=_cparams("parallel"),
        name="mixer_d",
    )(proj, proj, proj, proj, proj, conv_w)


def _gmlp_kernel(u_ref, v_ref, lg_ref, lb_ref, ws_ref, bs_ref, o_ref):
    tl = u_ref.shape[0]
    gd = BRANCH_W // C_GROUPS
    row = lax.broadcasted_iota(jnp.int32, (C_CHUNK, C_CHUNK), 0)
    col = lax.broadcasted_iota(jnp.int32, (C_CHUNK, C_CHUNK), 1)
    causal = row >= col
    for g in range(C_GROUPS):
        w_m = jnp.where(causal, ws_ref[g], 0.0).astype(MXU_DTYPE)
        bias = bs_ref[:, g:g + 1]
        gain = lg_ref[:, g * gd:(g + 1) * gd]
        shift = lb_ref[:, g * gd:(g + 1) * gd]
        for c in range(tl // C_CHUNK):
            rows = slice(c * C_CHUNK, (c + 1) * C_CHUNK)
            cols = slice(g * gd, (g + 1) * gd)
            v = jax.nn.gelu(v_ref[rows, cols])
            mu = jnp.mean(v, axis=-1, keepdims=True)
            vc = v - mu
            var = jnp.mean(vc * vc, axis=-1, keepdims=True)
            vn = vc * lax.rsqrt(var + EPS) * gain + shift
            s = _dot(w_m, vn.astype(MXU_DTYPE)) + bias
            o_ref[rows, cols] = (jax.nn.gelu(u_ref[rows, cols]) * s).astype(o_ref.dtype)


def spatial_gating_mixer(proj, ln_gain, ln_bias, w_s, b_s, tl=256):
    t = proj.shape[0]
    w = BRANCH_W
    return pl.pallas_call(
        _gmlp_kernel,
        grid=(t // tl,),
        in_specs=[pl.BlockSpec((tl, w), lambda i: (i, P_CU // w)),
                  pl.BlockSpec((tl, w), lambda i: (i, P_CV // w)),
                  pl.BlockSpec((1, w), lambda i: (0, 0)),
                  pl.BlockSpec((1, w), lambda i: (0, 0)),
                  pl.BlockSpec((C_GROUPS, C_CHUNK, C_CHUNK), lambda i: (0, 0, 0)),
                  pl.BlockSpec((C_CHUNK, C_GROUPS), lambda i: (0, 0))],
        out_specs=pl.BlockSpec((tl, w), lambda i: (i, 0)),
        out_shape=jax.ShapeDtypeStruct((t, w), MXU_DTYPE),
        compiler_params=_cparams("parallel"),
        name="mixer_c",
    )(proj, proj, ln_gain.reshape(1, w), ln_bias.reshape(1, w), w_s, b_s.T)


def _a_prep_kernel(k_ref, v_ref, sm_ref, gk_ref, gi_ref, kn_ref, vt_ref, ki_ref):
    k = k_ref[...]
    kn = k * lax.rsqrt(jnp.mean(k * k, axis=-1, keepdims=True) + EPS) * gk_ref[...]
    kn_ref[...] = kn.astype(kn_ref.dtype)
    vt_ref[...] = _nt_dot(_eye(HEAD_DIM, MXU_DTYPE), v_ref[...].astype(MXU_DTYPE)).astype(vt_ref.dtype)
    ik = sm_ref[:, SM_IK:SM_IK + IDX_DIM]
    ikn = ik * lax.rsqrt(jnp.mean(ik * ik, axis=-1, keepdims=True) + EPS) * gi_ref[...]
    ki_ref[...] = ikn.astype(ki_ref.dtype)


def dsa_prep(proj, k_gain, ik_gain, tl=512):
    t = proj.shape[0]
    hd = HEAD_DIM
    return pl.pallas_call(
        _a_prep_kernel,
        grid=(t // tl,),
        in_specs=[pl.BlockSpec((tl, hd), lambda i: (i, P_AK // hd)),
                  pl.BlockSpec((tl, hd), lambda i: (i, P_AV // hd)),
                  pl.BlockSpec((tl, hd), lambda i: (i, P_SM // hd)),
                  pl.BlockSpec((1, hd), lambda i: (0, 0)),
                  pl.BlockSpec((1, IDX_DIM), lambda i: (0, 0))],
        out_specs=[pl.BlockSpec((tl, hd), lambda i: (i, 0)),
                   pl.BlockSpec((hd, tl), lambda i: (0, i)),
                   pl.BlockSpec((tl, IDX_DIM), lambda i: (i, 0))],
        out_shape=[jax.ShapeDtypeStruct((t, hd), MXU_DTYPE),
                   jax.ShapeDtypeStruct((hd, t), MXU_DTYPE),
                   jax.ShapeDtypeStruct((t, IDX_DIM), MXU_DTYPE)],
        compiler_params=_cparams("parallel"),
        name="mixer_a_prep",
    )(proj, proj, proj, k_gain.reshape(1, hd), ik_gain.reshape(1, IDX_DIM))


def _dsa_kernel(q_ref, iq_ref, sm_ref, kn_ref, vt_ref, ki_ref, gq_ref, o_ref,
                keys_ref, qt_ref, qit_ref, m_ref, l_ref, acc_ref, *, topk, ck, seq_len):
    qb = Q_BLOCK
    bf = MXU_DTYPE
    f32 = jnp.float32
    i = pl.program_id(1)
    n_chunks = (i * qb + qb + ck - 1) // ck

    def chunk_start(c):
        return pl.multiple_of(c * ck, ck)

    def col_sum(x):
        return jnp.sum(jnp.sum(x.reshape(4, ck // 32, 8, qb), axis=1), axis=0)

    gq = gq_ref[...]
    eye_d = _eye(HEAD_DIM, bf)
    for h in range(A_HEADS):
        qh = q_ref[:, h * HEAD_DIM:(h + 1) * HEAD_DIM]
        qn = qh * lax.rsqrt(jnp.mean(qh * qh, axis=-1, keepdims=True) + EPS) * gq
        qt_ref[:, h * qb:(h + 1) * qb] = _nt_dot(eye_d, qn.astype(bf)).astype(bf)
    eye_i = _eye(IDX_DIM, bf)
    for h in range(IDX_HEADS):
        qi = iq_ref[:, h * IDX_DIM:(h + 1) * IDX_DIM].astype(bf)
        qit_ref[:, h * qb:(h + 1) * qb] = _nt_dot(eye_i, qi).astype(bf)
    w_t = _nt_dot(_eye(IDX_HEADS, f32), sm_ref[:, SM_IW:SM_IW + IDX_HEADS] * IDX_W_SCALE,
                  precision=HIGHEST)

    q_pos = i * qb + lax.broadcasted_iota(jnp.int32, (ck, qb), 1)
    k_off = lax.broadcasted_iota(jnp.int32, (ck, qb), 0)

    def score_chunk(c, carry):
        start = chunk_start(c)
        kc = ki_ref[pl.ds(start, ck), :]
        acc = jnp.zeros((ck, qb), f32)
        for hp in range(IDX_HEADS // 2):
            r = _dot(kc, qit_ref[:, hp * 2 * qb:(hp + 1) * 2 * qb])
            for s in range(2):
                h = 2 * hp + s
                acc = acc + w_t[h:h + 1, :] * jnp.maximum(r[:, s * qb:(s + 1) * qb], 0.0)
        score = jnp.where(k_off + start <= q_pos, acc, -jnp.inf)
        bits = pltpu.bitcast(score, jnp.int32)
        keys_ref[pl.ds(start, ck), :] = bits ^ ((bits >> 31) & 0x7FFFFFFF)
        return carry

    lax.fori_loop(0, n_chunks, score_chunk, 0)

    def count(pred):
        def hits(c):
            start = chunk_start(c)
            return col_sum(jnp.where(pred(keys_ref[pl.ds(start, ck), :], k_off + start), 1.0, 0.0))

        def two_chunks(j, cnt):
            second = jnp.minimum(2 * j + 1, n_chunks - 1)
            valid = (2 * j + 1 < n_chunks).astype(f32)
            return cnt + hits(2 * j) + hits(second) * valid

        cnt = lax.fori_loop(0, (n_chunks + 1) // 2, two_chunks, jnp.zeros((8, qb), f32))
        return jnp.sum(cnt, axis=0, keepdims=True)

    def bit_step(b, tau_u):
        cand_u = tau_u | lax.shift_left(jnp.int32(1), 31 - b)
        cand_s = cand_u ^ INT_MIN
        total = count(lambda kk, _: kk >= cand_s)
        return jnp.where(total >= float(topk), cand_u, tau_u)

    tau_u = lax.fori_loop(0, 32, bit_step, jnp.zeros((1, qb), jnp.int32))
    tau = jnp.maximum(tau_u ^ INT_MIN, KEY_NEG_INF + 1)

    need = float(topk) - count(lambda kk, _: kk > tau)
    n_equal = count(lambda kk, _: kk == tau)
    has_tie = jnp.max(jnp.where(n_equal > need, 1.0, 0.0)) > 0.5

    @pl.when(has_tie)
    def _():
        nbits = seq_len.bit_length()

        def tie_bit(b, x):
            cand = x | lax.shift_left(jnp.int32(1), nbits - 1 - b)
            below = count(lambda kk, kidx: jnp.where(kk == tau, kidx, cand) < cand)
            return jnp.where(below < need, cand, x)

        x = lax.fori_loop(0, nbits, tie_bit, jnp.zeros((1, qb), jnp.int32))

        def demote(c, carry):
            start = chunk_start(c)
            kk = keys_ref[pl.ds(start, ck), :]
            drop = jnp.where(kk == tau, k_off + start, x) > x
            keys_ref[pl.ds(start, ck), :] = jnp.where(drop, KEY_NEG_INF, kk)
            return carry

        lax.fori_loop(0, n_chunks, demote, 0)

    m_ref[...] = jnp.full(m_ref.shape, NEG_BIG, f32)
    l_ref[...] = jnp.zeros(l_ref.shape, f32)
    acc_ref[...] = jnp.zeros(acc_ref.shape, f32)
    c_exp = (HEAD_DIM ** -0.5) * 1.4426950408889634
    pairs = [slice(hp * 2 * qb, (hp + 1) * 2 * qb) for hp in range(A_HEADS // 2)]

    def attn_chunk(c, carry):
        start = chunk_start(c)
        kc = kn_ref[pl.ds(start, ck), :]
        vt = vt_ref[:, pl.ds(start, ck)]
        sel = keys_ref[pl.ds(start, ck), :] >= tau
        raw = [_dot(kc, qt_ref[:, cols]) for cols in pairs]
        probs, alphas = [], []
        for cols, s in zip(pairs, raw):
            s = jnp.concatenate([jnp.where(sel, s[:, :qb], NEG_BIG), jnp.where(sel, s[:, qb:], NEG_BIG)], axis=1)
            m_old = m_ref[:, cols]
            m_new = jnp.maximum(m_old, jnp.max(s, axis=0, keepdims=True))
            alpha = jnp.exp2((m_old - m_new) * c_exp)
            p = jnp.exp2((s - m_new) * c_exp)
            l_ref[:, cols] = alpha * l_ref[:, cols] + jnp.sum(p, axis=0, keepdims=True)
            m_ref[:, cols] = m_new
            probs.append(p.astype(bf))
            alphas.append(alpha)
        for cols, p, alpha in zip(pairs, probs, alphas):
            acc_ref[:, cols] = alpha * acc_ref[:, cols] + _dot(vt, p)
        return carry

    lax.fori_loop(0, n_chunks, attn_chunk, 0)

    for h in range(A_HEADS):
        cols = slice(h * qb, (h + 1) * qb)
        o_t = acc_ref[:, cols] / l_ref[:, cols]
        o_ref[:, h * HEAD_DIM:(h + 1) * HEAD_DIM] = o_t.T.astype(o_ref.dtype)


def dsa_mixer(proj, kn, vt, ki, q_gain, batch, seq_len, ck=512):
    t = proj.shape[0]
    qb = Q_BLOCK
    nq = seq_len // qb
    ck = min(ck, seq_len)
    topk = min(TOPK_MAX, seq_len // 4)
    w = BRANCH_W
    return pl.pallas_call(
        functools.partial(_dsa_kernel, topk=topk, ck=ck, seq_len=seq_len),
        grid=(batch, nq),
        in_specs=[pl.BlockSpec((qb, w), lambda b, i: (b * nq + i, P_AQ // w)),
                  pl.BlockSpec((qb, w), lambda b, i: (b * nq + i, P_IQ // w)),
                  pl.BlockSpec((qb, HEAD_DIM), lambda b, i: (b * nq + i, P_SM // HEAD_DIM)),
                  pl.BlockSpec((seq_len, HEAD_DIM), lambda b, i: (b, 0)),
                  pl.BlockSpec((HEAD_DIM, seq_len), lambda b, i: (0, b)),
                  pl.BlockSpec((seq_len, IDX_DIM), lambda b, i: (b, 0)),
                  pl.BlockSpec((1, HEAD_DIM), lambda b, i: (0, 0))],
        out_specs=pl.BlockSpec((qb, w), lambda b, i: (b * nq + i, 0)),
        out_shape=jax.ShapeDtypeStruct((t, w), MXU_DTYPE),
        scratch_shapes=[pltpu.VMEM((seq_len, qb), jnp.int32),
                        pltpu.VMEM((HEAD_DIM, A_HEADS * qb), MXU_DTYPE),
                        pltpu.VMEM((IDX_DIM, IDX_HEADS * qb), MXU_DTYPE),
                        pltpu.VMEM((1, A_HEADS * qb), jnp.float32),
                        pltpu.VMEM((1, A_HEADS * qb), jnp.float32),
                        pltpu.VMEM((HEAD_DIM, A_HEADS * qb), jnp.float32)],
        compiler_params=_cparams("parallel", "arbitrary"),
        name="mixer_a",
    )(proj, proj, proj, kn, vt, ki, q_gain.reshape(1, HEAD_DIM))


def _b_prep_kernel(q_ref, k_ref, v_ref, qp_ref, kp_ref, vp_ref, sm_ref, cw_ref, alog_ref, dtb_ref,
                   qo_ref, ko_ref, vo_ref, g_ref, beta_ref, ext_ref, *, tiles_per_seq):
    tl = q_ref.shape[0]
    w = BRANCH_W
    first = (pl.program_id(0) % tiles_per_seq) == 0
    parts = ((q_ref, qp_ref, qo_ref, True), (k_ref, kp_ref, ko_ref, True), (v_ref, vp_ref, vo_ref, False))
    for n, (cur, prev, out, normed) in enumerate(parts):
        ext_ref[0:HALO, :] = jnp.where(first, 0.0, prev[...])
        ext_ref[HALO:HALO + tl, :] = cur[...]
        cw = cw_ref[:, n * w:(n + 1) * w]
        acc = cw[B_CONV - 1:B_CONV, :] * cur[...]
        for j in range(B_CONV - 1):
            off = HALO - (B_CONV - 1) + j
            acc = acc + cw[j:j + 1, :] * ext_ref[off:off + tl, :]
        y = acc * jax.nn.sigmoid(acc)
        if normed:
            for h in range(B_HEADS):
                cols = slice(h * HEAD_DIM, (h + 1) * HEAD_DIM)
                yh = y[:, cols]
                out[:, cols] = yh * lax.rsqrt(jnp.sum(yh * yh, axis=-1, keepdims=True) + EPS)
        else:
            out[...] = y
    a = sm_ref[:, SM_BA:SM_BA + B_HEADS] + dtb_ref[...]
    softplus = jnp.maximum(a, 0.0) + jnp.log(1.0 + jnp.exp(-jnp.abs(a)))
    g_ref[...] = -jnp.exp(alog_ref[...]) * softplus
    beta_ref[...] = jax.nn.sigmoid(sm_ref[:, SM_BB:SM_BB + B_HEADS])


def deltanet_prep(proj, conv_w, a_log, dt_bias, seq_len, tl=256):
    t = proj.shape[0]
    w = BRANCH_W
    cb = P_BQ // w
    cur = lambda c: pl.BlockSpec((tl, w), lambda i: (i, c))
    prev = lambda c: pl.BlockSpec((HALO, w), lambda i: (jnp.maximum(i * (tl // HALO) - 1, 0), c))
    row = lambda width: pl.BlockSpec((tl, width), lambda i: (i, 0))
    return pl.pallas_call(
        functools.partial(_b_prep_kernel, tiles_per_seq=seq_len // tl),
        grid=(t // tl,),
        in_specs=[cur(cb), cur(cb + 1), cur(cb + 2), prev(cb), prev(cb + 1), prev(cb + 2),
                  pl.BlockSpec((tl, HEAD_DIM), lambda i: (i, P_SM // HEAD_DIM)),
                  pl.BlockSpec((B_CONV, 3 * w), lambda i: (0, 0)),
                  pl.BlockSpec((1, B_HEADS), lambda i: (0, 0)),
                  pl.BlockSpec((1, B_HEADS), lambda i: (0, 0))],
        out_specs=[row(w), row(w), row(w), row(B_HEADS), row(B_HEADS)],
        out_shape=[jax.ShapeDtypeStruct((t, w), jnp.float32)] * 3
        + [jax.ShapeDtypeStruct((t, B_HEADS), jnp.float32)] * 2,
        scratch_shapes=[pltpu.VMEM((tl + HALO, w), jnp.float32)],
        compiler_params=_cparams("parallel"),
        name="mixer_b_prep",
    )(proj, proj, proj, proj, proj, proj, proj, conv_w, a_log.reshape(1, B_HEADS), dt_bias.reshape(1, B_HEADS))


def _delta_kernel(q_ref, k_ref, v_ref, g_ref, beta_ref, z_ref, gain_ref, o_ref, state_ref):
    nb = q_ref.shape[0]
    cs = B_CHUNK
    bf = MXU_DTYPE
    f32 = jnp.float32

    @pl.when(pl.program_id(0) == 0)
    def _():
        state_ref[...] = jnp.zeros(state_ref.shape, f32)

    row = lax.broadcasted_iota(jnp.int32, (cs, cs), 0)
    col = lax.broadcasted_iota(jnp.int32, (cs, cs), 1)
    incl = row >= col
    strict = row > col
    eye_c = (row == col).astype(f32)
    eye_h = _eye(B_HEADS, f32)
    eye_d = _eye(HEAD_DIM, bf)
    tril = incl.astype(f32)
    gain = gain_ref[...]
    qscale = HEAD_DIM ** -0.5

    chains = []
    for b in range(nb):
        gc = _dot(tril, g_ref[b], precision=HIGHEST)
        gc_t = _nt_dot(eye_h, gc, precision=HIGHEST)
        eg = jnp.exp(gc)
        g_last = gc[cs - 1:cs, :]
        ek = jnp.exp(g_last - gc)
        e_last = jnp.exp(g_last)
        beta = beta_ref[b]
        for h in range(B_HEADS):
            cols = slice(h * HEAD_DIM, (h + 1) * HEAD_DIM)
            qh = q_ref[b, :, cols] * qscale
            kh = k_ref[b, :, cols]
            bh = beta[:, h:h + 1]
            kb = kh * bh
            decay = jnp.exp(jnp.where(incl, gc[:, h:h + 1] - gc_t[h:h + 1, :], -jnp.inf))
            chains.append(dict(
                b=b, h=h, cols=cols, idx=b * B_HEADS + h, decay=decay,
                q_b=qh.astype(bf), k_b=kh.astype(bf), kb_b=kb.astype(bf),
                vb_b=(v_ref[b, :, cols] * bh).astype(bf),
                kbe_b=(kb * eg[:, h:h + 1]).astype(bf),
                qdec_b=(qh * eg[:, h:h + 1]).astype(bf),
                kdec_b=(kh * ek[:, h:h + 1]).astype(bf),
                e_last=e_last[:, h:h + 1]))

    for ch in chains:
        ch["lower"] = jnp.where(strict, _nt_dot(ch["kb_b"], ch["k_b"]) * ch["decay"], 0.0)
    for ch in chains:
        ch["attn_b"] = jnp.where(incl, _nt_dot(ch["q_b"], ch["k_b"]) * ch["decay"], 0.0).astype(bf)
    for ch in chains:
        ch["inv"] = eye_c - ch["lower"]
        ch["pw"] = ch["lower"]
    for _ in range(5):
        for ch in chains:
            ch["pw"] = _dot3(ch["pw"], ch["pw"])
        for ch in chains:
            ch["inv"] = ch["inv"] + _dot3(ch["inv"], ch["pw"])
    for ch in chains:
        ch["inv_b"] = ch["inv"].astype(bf)
        ch["u"] = _dot(ch["inv_b"], ch["vb_b"])
    for ch in chains:
        ch["w_b"] = _dot(ch["inv_b"], ch["kbe_b"]).astype(bf)
    for ch in chains:
        ch["state_b"] = state_ref[ch["idx"]].astype(bf)
        ch["vnew_b"] = (ch["u"] - _dot(ch["w_b"], ch["state_b"])).astype(bf)
    for ch in chains:
        ch["out"] = _dot(ch["qdec_b"], ch["state_b"]) + _dot(ch["attn_b"], ch["vnew_b"])
    for ch in chains:
        ch["kdec_t"] = _nt_dot(eye_d, ch["kdec_b"]).astype(bf)
    for ch in chains:
        state_ref[ch["idx"]] = state_ref[ch["idx"]] * ch["e_last"] + _dot(ch["kdec_t"], ch["vnew_b"])
    for ch in chains:
        out = ch["out"]
        on = out * lax.rsqrt(jnp.mean(out * out, axis=-1, keepdims=True) + EPS) * gain
        z = z_ref[ch["b"], :, ch["cols"]]
        o_ref[ch["b"], :, ch["cols"]] = (on * (z * jax.nn.sigmoid(z))).astype(o_ref.dtype)


def deltanet_mixer(proj, q, k, v, g, beta, out_gain, batch, seq_len):
    w = BRANCH_W
    cs = B_CHUNK
    r3 = lambda a: a.reshape(batch, seq_len, a.shape[-1])
    blk = lambda width, cb=0: pl.BlockSpec((batch, cs, width), lambda c: (0, c, cb))
    out = pl.pallas_call(
        _delta_kernel,
        grid=(seq_len // cs,),
        in_specs=[blk(w), blk(w), blk(w), blk(B_HEADS), blk(B_HEADS), blk(w, P_BZ // w),
                  pl.BlockSpec((1, HEAD_DIM), lambda c: (0, 0))],
        out_specs=blk(w),
        out_shape=jax.ShapeDtypeStruct((batch, seq_len, w), MXU_DTYPE),
        scratch_shapes=[pltpu.VMEM((batch * B_HEADS, HEAD_DIM, HEAD_DIM), jnp.float32)],
        compiler_params=_cparams("arbitrary"),
        name="mixer_b",
    )(r3(q), r3(k), r3(v), r3(g), r3(beta), r3(proj), out_gain.reshape(1, HEAD_DIM))
    return out.reshape(batch * seq_len, w)


def _merge_kernel(oa_ref, ob_ref, oc_ref, od_ref, g_ref, wb_ref, wg_ref, o_ref):
    lat = g_ref[...].astype(MXU_DTYPE)
    acc = None
    for n, o in enumerate((oa_ref, ob_ref, oc_ref, od_ref)):
        y = _dot(o[...], wb_ref[n])
        gate = jax.nn.sigmoid(_dot(lat, wg_ref[n]))
        acc = gate * y if acc is None else acc + gate * y
    o_ref[...] = acc.astype(o_ref.dtype)


def merge_branches(outs, proj, w_branch, w_gate, layer, tm=1024, tn=512):
    t = proj.shape[0]
    w = BRANCH_W
    tm = min(tm, t)
    n_br = len(outs)
    return pl.pallas_call(
        _merge_kernel,
        grid=(t // tm, D_MODEL // tn),
        in_specs=[pl.BlockSpec((tm, w), lambda i, j: (i, 0))] * n_br
        + [pl.BlockSpec((tm, GATE_RANK), lambda i, j: (i, P_G // GATE_RANK)),
           pl.BlockSpec((None, n_br, w, tn), lambda i, j: (layer, 0, 0, j)),
           pl.BlockSpec((None, n_br, GATE_RANK, tn), lambda i, j: (layer, 0, 0, j))],
        out_specs=pl.BlockSpec((tm, tn), lambda i, j: (i, j)),
        out_shape=jax.ShapeDtypeStruct((t, D_MODEL), MXU_DTYPE),
        compiler_params=_cparams("parallel", "parallel"),
        name="merge",
    )(*outs, proj, w_branch, w_gate)


N_IN = 11872
W_IN_SEGMENTS = (
    (0, 1024, P_AQ),
    (1280, 1024, P_IQ),
    (2384, 3072, P_BQ),
    (5456, 1024, P_BZ),
    (6496, 2048, P_CU),
    (8544, 3072, P_DH),
    (11616, 256, P_G),
    (1024, 128, P_AK),
    (1152, 128, P_AV),
    (2304, 64, P_SM + SM_IK),
    (2368, 16, P_SM + SM_IW),
    (6480, 8, P_SM + SM_BA),
    (6488, 8, P_SM + SM_BB),
)
SM_USED = SM_BB + 8


def _permute_w_in_kernel(w_ref, o_ref):
    for src, width, dst in W_IN_SEGMENTS:
        o_ref[:, dst:dst + width] = w_ref[:, src:src + width].astype(o_ref.dtype)
    o_ref[:, P_SM + SM_USED:N_P] = jnp.zeros((o_ref.shape[0], N_P - P_SM - SM_USED), o_ref.dtype)


def permute_w_in(w, tr=256):
    depth, d, n_in = w.shape
    assert n_in == N_IN
    return pl.pallas_call(
        _permute_w_in_kernel,
        grid=(depth, d // tr),
        in_specs=[pl.BlockSpec((None, tr, N_IN), lambda l, i: (l, i, 0))],
        out_specs=pl.BlockSpec((None, tr, N_P), lambda l, i: (l, i, 0)),
        out_shape=jax.ShapeDtypeStruct((depth, d, N_P), MXU_DTYPE),
        compiler_params=_cparams("parallel", "parallel"),
        name="permute_w_in",
    )(w)


def kernel(x, mix_norm, w_in, a_q_norm, a_k_norm, a_idx_k_norm, b_conv, b_a_log, b_dt_bias, b_out_norm,
           c_ln_gain, c_ln_bias, c_spatial_w, c_spatial_b, d_conv, w_branch, w_gate_up, w_out, ffn_norm,
           w_ff1, w_ff2):
    bn, seq_len, d = x.shape
    depth = w_in.shape[0]
    bf = MXU_DTYPE
    w_in_p = permute_w_in(w_in)
    w_branch_b = w_branch.astype(bf).reshape(depth, 4, BRANCH_W, d)
    w_gate_b = w_gate_up.astype(bf)
    w_ff2_b = w_ff2.astype(bf)

    xt = x.reshape(bn * seq_len, d)
    xb, ssq = norm_stats(xt, mix_norm[0])
    for l in range(depth):
        proj = matmul(xb, w_in_p, l, row_ssq=ssq)
        kn, vt, ki = dsa_prep(proj, a_k_norm[l], a_idx_k_norm[l])
        o_a = dsa_mixer(proj, kn, vt, ki, a_q_norm[l], bn, seq_len)
        bq, bk, bv, bg, bbeta = deltanet_prep(proj, b_conv[l], b_a_log[l], b_dt_bias[l], seq_len)
        o_b = deltanet_mixer(proj, bq, bk, bv, bg, bbeta, b_out_norm[l], bn, seq_len)
        o_c = spatial_gating_mixer(proj, c_ln_gain[l], c_ln_bias[l], c_spatial_w[l], c_spatial_b[l])
        o_d = short_conv_mixer(proj, d_conv[l], seq_len)
        merged = merge_branches((o_a, o_b, o_c, o_d), proj, w_branch_b, w_gate_b, l)
        xt, xb, ssq = matmul(merged, w_out, l, epilogue="residual", residual=xt, next_gain=ffn_norm[l], tn=512)
        mid = matmul(xb, w_ff1, l, epilogue="relu2", row_ssq=ssq, out_dtype=bf, tn=512)
        ff2_tiles = dict(tm=512, tn=256, tk=w_ff2_b.shape[1])
        if l + 1 < depth:
            xt, xb, ssq = matmul(mid, w_ff2_b, l, epilogue="residual", residual=xt, next_gain=mix_norm[l + 1],
                                 **ff2_tiles)
        else:
            xt = matmul(mid, w_ff2_b, l, epilogue="residual", residual=xt, **ff2_tiles)
    return xt.reshape(bn, seq_len, d)
```

```python
import functools

import jax
import jax.numpy as jnp
from jax import lax
from jax.experimental import pallas as pl
from jax.experimental.pallas import tpu as pltpu

D_MODEL = 4096
HEAD_DIM = 128
BRANCH_W = 1024
A_HEADS = 8
IDX_HEADS = 16
IDX_DIM = 64
IDX_W_SCALE = (IDX_HEADS * IDX_DIM) ** -0.5
TOPK_MAX = 256
Q_BLOCK = 128
B_HEADS = 8
B_CONV = 4
B_CHUNK = 64
C_CHUNK = 128
C_GROUPS = 8
D_CONV = 3
GATE_RANK = 256
EPS = 1e-6

P_AQ, P_IQ, P_BQ, P_BZ, P_CU, P_CV, P_DH, P_DB, P_DC = 0, 1024, 2048, 5120, 6144, 7168, 8192, 9216, 10240
P_G, P_AK, P_AV, P_SM = 11264, 11520, 11648, 11776
N_P = 11904
SM_IK, SM_IW, SM_BA, SM_BB = 0, 64, 80, 88

MXU_DTYPE = jnp.bfloat16
LANES = 128
HALO = 8
VMEM_LIMIT = 56 * 1024 * 1024
NEG_BIG = -1e30
INT_MIN = -(2 ** 31)
KEY_NEG_INF = INT_MIN + 0x7FFFFF
HIGHEST = lax.Precision.HIGHEST


def _cparams(*sem):
    return pltpu.CompilerParams(dimension_semantics=sem, vmem_limit_bytes=VMEM_LIMIT)


def _nt_dot(a, b, precision=None):
    return lax.dot_general(a, b, (((1,), (1,)), ((), ())), precision=precision,
                           preferred_element_type=jnp.float32)


def _dot(a, b, precision=None):
    return jnp.dot(a, b, precision=precision, preferred_element_type=jnp.float32)


def _dot3(a, b):
    a_hi, b_hi = a.astype(MXU_DTYPE), b.astype(MXU_DTYPE)
    a_lo = (a - a_hi.astype(jnp.float32)).astype(MXU_DTYPE)
    b_lo = (b - b_hi.astype(jnp.float32)).astype(MXU_DTYPE)
    return _dot(a_hi, b_hi) + (_dot(a_hi, b_lo) + _dot(a_lo, b_hi))


def _eye(n, dtype):
    r = lax.broadcasted_iota(jnp.int32, (n, n), 0)
    c = lax.broadcasted_iota(jnp.int32, (n, n), 1)
    return (r == c).astype(dtype)


def _lane_partial_sq(x):
    sq = x * x
    out = sq[:, 0:LANES]
    for g in range(1, x.shape[1] // LANES):
        out = out + sq[:, g * LANES:(g + 1) * LANES]
    return out


def _norm_stats_kernel(x_ref, g_ref, xb_ref, ssq_ref):
    x = x_ref[...]
    xb_ref[...] = (x * g_ref[...]).astype(xb_ref.dtype)
    ssq_ref[...] = _lane_partial_sq(x)


def norm_stats(x, gain, tl=256):
    t, d = x.shape
    return pl.pallas_call(
        _norm_stats_kernel,
        grid=(t // tl,),
        in_specs=[pl.BlockSpec((tl, d), lambda i: (i, 0)),
                  pl.BlockSpec((1, d), lambda i: (0, 0))],
        out_specs=[pl.BlockSpec((tl, d), lambda i: (i, 0)),
                   pl.BlockSpec((tl, LANES), lambda i: (i, 0))],
        out_shape=[jax.ShapeDtypeStruct((t, d), MXU_DTYPE),
                   jax.ShapeDtypeStruct((t, LANES), jnp.float32)],
        compiler_params=_cparams("parallel"),
        name="norm_stats",
    )(x, gain.reshape(1, d))


def _mm_kernel(*refs, nk, kdim, epilogue, row_scaled, emit_stats):
    refs = list(refs)
    a_ref, w_ref = refs.pop(0), refs.pop(0)
    ssq_in_ref = refs.pop(0) if row_scaled else None
    r_ref = refs.pop(0) if epilogue == "residual" else None
    g_next_ref = refs.pop(0) if emit_stats else None
    o_ref = refs.pop(0)
    xb_ref, ssq_out_ref = (refs.pop(0), refs.pop(0)) if emit_stats else (None, None)
    scratch = refs
    part = _dot(a_ref[...], w_ref[...])

    def finish(acc):
        if row_scaled:
            acc = acc * lax.rsqrt(jnp.sum(ssq_in_ref[...], axis=-1, keepdims=True) * (1.0 / kdim) + EPS)
        if epilogue == "residual":
            out = r_ref[...] + acc
            o_ref[...] = out
            if emit_stats:
                xb_ref[...] = (out * g_next_ref[...]).astype(xb_ref.dtype)
                j = pl.program_id(1)

                @pl.when(j == 0)
                def _():
                    ssq_out_ref[...] = _lane_partial_sq(out)

                @pl.when(j > 0)
                def _():
                    ssq_out_ref[...] += _lane_partial_sq(out)
        elif epilogue == "relu2":
            o_ref[...] = jnp.square(jnp.maximum(acc, 0.0)).astype(o_ref.dtype)
        else:
            o_ref[...] = acc.astype(o_ref.dtype)

    if nk == 1:
        finish(part)
    else:
        acc_ref = scratch[0]
        k = pl.program_id(2)

        @pl.when(k == 0)
        def _():
            acc_ref[...] = part

        @pl.when(k > 0)
        def _():
            acc_ref[...] += part

        @pl.when(k == nk - 1)
        def _():
            finish(acc_ref[...])


def matmul(a, w_stack, layer, *, epilogue="none", residual=None, row_ssq=None, next_gain=None,
           k_range=None, out_dtype=jnp.float32, tm=1024, tn=1024, tk=4096):
    emit_stats = next_gain is not None
    m, kdim = a.shape
    n = w_stack.shape[2]
    k_start, k_size = k_range if k_range is not None else (0, kdim)
    tm, tn, tk = min(tm, m), min(tn, n), min(tk, k_size)
    nk = k_size // tk
    k0 = k_start // tk
    assert k_size % tk == 0 and k_start % tk == 0
    tile = pl.BlockSpec((tm, tn), lambda i, j, k: (i, j))
    stats = pl.BlockSpec((tm, LANES), lambda i, j, k: (i, 0))
    in_specs = [pl.BlockSpec((tm, tk), lambda i, j, k: (i, k0 + k)),
                pl.BlockSpec((None, tk, tn), lambda i, j, k: (layer, k0 + k, j))]
    args = [a, w_stack]
    if row_ssq is not None:
        in_specs.append(stats)
        args.append(row_ssq)
    if epilogue == "residual":
        in_specs.append(tile)
        args.append(residual)
    if emit_stats:
        in_specs.append(pl.BlockSpec((1, tn), lambda i, j, k: (0, j)))
        args.append(next_gain.reshape(1, n))
    out_specs, out_shape = [tile], [jax.ShapeDtypeStruct((m, n), out_dtype)]
    if emit_stats:
        out_specs += [tile, stats]
        out_shape += [jax.ShapeDtypeStruct((m, n), MXU_DTYPE), jax.ShapeDtypeStruct((m, LANES), jnp.float32)]
    scratch = [pltpu.VMEM((tm, tn), jnp.float32)] if nk > 1 else []
    out = pl.pallas_call(
        functools.partial(_mm_kernel, nk=nk, kdim=kdim, epilogue=epilogue, row_scaled=row_ssq is not None,
                          emit_stats=emit_stats),
        grid=(m // tm, pl.cdiv(n, tn), nk),
        in_specs=in_specs,
        out_specs=out_specs,
        out_shape=out_shape,
        scratch_shapes=scratch,
        compiler_params=_cparams("parallel", "arbitrary", "arbitrary"),
        name="matmul_" + epilogue,
    )(*args)
    return out if emit_stats else out[0]


def _dconv_kernel(h_ref, b_ref, c_ref, hp_ref, cp_ref, w_ref, o_ref, ext_ref, *, tiles_per_seq):
    tl = h_ref.shape[0]
    first = (pl.program_id(0) % tiles_per_seq) == 0
    ext_ref[0:HALO, :] = jnp.where(first, 0.0, cp_ref[...] * hp_ref[...])
    p = c_ref[...] * h_ref[...]
    ext_ref[HALO:HALO + tl, :] = p
    w = w_ref[...]
    acc = w[2:3, :] * p
    acc = acc + w[1:2, :] * ext_ref[HALO - 1:HALO - 1 + tl, :]
    acc = acc + w[0:1, :] * ext_ref[HALO - 2:HALO - 2 + tl, :]
    o_ref[...] = (b_ref[...] * acc).astype(o_ref.dtype)


def short_conv_mixer(proj, conv_w, seq_len, tl=512):
    t = proj.shape[0]
    w = BRANCH_W
    cur = lambda cb: pl.BlockSpec((tl, w), lambda i: (i, cb))
    prev = lambda cb: pl.BlockSpec((HALO, w), lambda i: (jnp.maximum(i * (tl // HALO) - 1, 0), cb))
    return pl.pallas_call(
        functools.partial(_dconv_kernel, tiles_per_seq=seq_len // tl),
        grid=(t // tl,),
        in_specs=[cur(P_DH // w), cur(P_DB // w), cur(P_DC // w), prev(P_DH // w), prev(P_DC // w),
                  pl.BlockSpec((D_CONV, w), lambda i: (0, 0))],
        out_specs=pl.BlockSpec((tl, w), lambda i: (i, 0)),
        out_shape=jax.ShapeDtypeStruct((t, w), MXU_DTYPE),
        scratch_shapes=[pltpu.VMEM((tl + HALO, w), jnp.float32)],
        compiler_params=_cparams("parallel"),
        name="mixer_d",
    )(proj, proj, proj, proj, proj, conv_w)


def _gmlp_kernel(u_ref, v_ref, lg_ref, lb_ref, ws_ref, bs_ref, o_ref):
    tl = u_ref.shape[0]
    gd = BRANCH_W // C_GROUPS
    row = lax.broadcasted_iota(jnp.int32, (C_CHUNK, C_CHUNK), 0)
    col = lax.broadcasted_iota(jnp.int32, (C_CHUNK, C_CHUNK), 1)
    causal = row >= col
    for g in range(C_GROUPS):
        w_m = jnp.where(causal, ws_ref[g], 0.0).astype(MXU_DTYPE)
        bias = bs_ref[:, g:g + 1]
        gain = lg_ref[:, g * gd:(g + 1) * gd]
        shift = lb_ref[:, g * gd:(g + 1) * gd]
        for c in range(tl // C_CHUNK):
            rows = slice(c * C_CHUNK, (c + 1) * C_CHUNK)
            cols = slice(g * gd, (g + 1) * gd)
            v = jax.nn.gelu(v_ref[rows, cols])
            mu = jnp.mean(v, axis=-1, keepdims=True)
            vc = v - mu
            var = jnp.mean(vc * vc, axis=-1, keepdims=True)
            vn = vc * lax.rsqrt(var + EPS) * gain + shift
            s = _dot(w_m, vn.astype(MXU_DTYPE)) + bias
            o_ref[rows, cols] = (jax.nn.gelu(u_ref[rows, cols]) * s).astype(o_ref.dtype)


def spatial_gating_mixer(proj, ln_gain, ln_bias, w_s, b_s, tl=256):
    t = proj.shape[0]
    w = BRANCH_W
    return pl.pallas_call(
        _gmlp_kernel,
        grid=(t // tl,),
        in_specs=[pl.BlockSpec((tl, w), lambda i: (i, P_CU // w)),
                  pl.BlockSpec((tl, w), lambda i: (i, P_CV // w)),
                  pl.BlockSpec((1, w), lambda i: (0, 0)),
                  pl.BlockSpec((1, w), lambda i: (0, 0)),
                  pl.BlockSpec((C_GROUPS, C_CHUNK, C_CHUNK), lambda i: (0, 0, 0)),
                  pl.BlockSpec((C_CHUNK, C_GROUPS), lambda i: (0, 0))],
        out_specs=pl.BlockSpec((tl, w), lambda i: (i, 0)),
        out_shape=jax.ShapeDtypeStruct((t, w), MXU_DTYPE),
        compiler_params=_cparams("parallel"),
        name="mixer_c",
    )(proj, proj, ln_gain.reshape(1, w), ln_bias.reshape(1, w), w_s, b_s.T)


def _a_prep_kernel(k_ref, v_ref, sm_ref, gk_ref, gi_ref, kn_ref, vt_ref, ki_ref):
    k = k_ref[...]
    kn = k * lax.rsqrt(jnp.mean(k * k, axis=-1, keepdims=True) + EPS) * gk_ref[...]
    kn_ref[...] = kn.astype(kn_ref.dtype)
    vt_ref[...] = _nt_dot(_eye(HEAD_DIM, MXU_DTYPE), v_ref[...].astype(MXU_DTYPE)).astype(vt_ref.dtype)
    ik = sm_ref[:, SM_IK:SM_IK + IDX_DIM]
    ikn = ik * lax.rsqrt(jnp.mean(ik * ik, axis=-1, keepdims=True) + EPS) * gi_ref[...]
    ki_ref[...] = ikn.astype(ki_ref.dtype)


def dsa_prep(proj, k_gain, ik_gain, tl=512):
    t = proj.shape[0]
    hd = HEAD_DIM
    return pl.pallas_call(
        _a_prep_kernel,
        grid=(t // tl,),
        in_specs=[pl.BlockSpec((tl, hd), lambda i: (i, P_AK // hd)),
                  pl.BlockSpec((tl, hd), lambda i: (i, P_AV // hd)),
                  pl.BlockSpec((tl, hd), lambda i: (i, P_SM // hd)),
                  pl.BlockSpec((1, hd), lambda i: (0, 0)),
                  pl.BlockSpec((1, IDX_DIM), lambda i: (0, 0))],
        out_specs=[pl.BlockSpec((tl, hd), lambda i: (i, 0)),
                   pl.BlockSpec((hd, tl), lambda i: (0, i)),
                   pl.BlockSpec((tl, IDX_DIM), lambda i: (i, 0))],
        out_shape=[jax.ShapeDtypeStruct((t, hd), MXU_DTYPE),
                   jax.ShapeDtypeStruct((hd, t), MXU_DTYPE),
                   jax.ShapeDtypeStruct((t, IDX_DIM), MXU_DTYPE)],
        compiler_params=_cparams("parallel"),
        name="mixer_a_prep",
    )(proj, proj, proj, k_gain.reshape(1, hd), ik_gain.reshape(1, IDX_DIM))


def _dsa_kernel(q_ref, iq_ref, sm_ref, kn_ref, vt_ref, ki_ref, gq_ref, o_ref,
                keys_ref, qt_ref, qit_ref, m_ref, l_ref, acc_ref, *, topk, ck, seq_len):
    qb = Q_BLOCK
    bf = MXU_DTYPE
    f32 = jnp.float32
    i = pl.program_id(1)
    n_chunks = (i * qb + qb + ck - 1) // ck

    def chunk_start(c):
        return pl.multiple_of(c * ck, ck)

    def col_sum(x):
        return jnp.sum(jnp.sum(x.reshape(4, ck // 32, 8, qb), axis=1), axis=0)

    gq = gq_ref[...]
    eye_d = _eye(HEAD_DIM, bf)
    for h in range(A_HEADS):
        qh = q_ref[:, h * HEAD_DIM:(h + 1) * HEAD_DIM]
        qn = qh * lax.rsqrt(jnp.mean(qh * qh, axis=-1, keepdims=True) + EPS) * gq
        qt_ref[:, h * qb:(h + 1) * qb] = _nt_dot(eye_d, qn.astype(bf)).astype(bf)
    eye_i = _eye(IDX_DIM, bf)
    for h in range(IDX_HEADS):
        qi = iq_ref[:, h * IDX_DIM:(h + 1) * IDX_DIM].astype(bf)
        qit_ref[:, h * qb:(h + 1) * qb] = _nt_dot(eye_i, qi).astype(bf)
    w_t = _nt_dot(_eye(IDX_HEADS, f32), sm_ref[:, SM_IW:SM_IW + IDX_HEADS] * IDX_W_SCALE,
                  precision=HIGHEST)

    q_pos = i * qb + lax.broadcasted_iota(jnp.int32, (ck, qb), 1)
    k_off = lax.broadcasted_iota(jnp.int32, (ck, qb), 0)

    def score_chunk(c, carry):
        start = chunk_start(c)
        kc = ki_ref[pl.ds(start, ck), :]
        acc = jnp.zeros((ck, qb), f32)
        for hp in range(IDX_HEADS // 2):
            r = _dot(kc, qit_ref[:, hp * 2 * qb:(hp + 1) * 2 * qb])
            for s in range(2):
                h = 2 * hp + s
                acc = acc + w_t[h:h + 1, :] * jnp.maximum(r[:, s * qb:(s + 1) * qb], 0.0)
        score = jnp.where(k_off + start <= q_pos, acc, -jnp.inf)
        bits = pltpu.bitcast(score, jnp.int32)
        keys_ref[pl.ds(start, ck), :] = bits ^ ((bits >> 31) & 0x7FFFFFFF)
        return carry

    lax.fori_loop(0, n_chunks, score_chunk, 0)

    def count(pred):
        def hits(c):
            start = chunk_start(c)
            return col_sum(jnp.where(pred(keys_ref[pl.ds(start, ck), :], k_off + start), 1.0, 0.0))

        def two_chunks(j, cnt):
            second = jnp.minimum(2 * j + 1, n_chunks - 1)
            valid = (2 * j + 1 < n_chunks).astype(f32)
            return cnt + hits(2 * j) + hits(second) * valid

        cnt = lax.fori_loop(0, (n_chunks + 1) // 2, two_chunks, jnp.zeros((8, qb), f32))
        return jnp.sum(cnt, axis=0, keepdims=True)

    def bit_step(b, tau_u):
        cand_u = tau_u | lax.shift_left(jnp.int32(1), 31 - b)
        cand_s = cand_u ^ INT_MIN
        total = count(lambda kk, _: kk >= cand_s)
        return jnp.where(total >= float(topk), cand_u, tau_u)

    tau_u = lax.fori_loop(0, 32, bit_step, jnp.zeros((1, qb), jnp.int32))
    tau = jnp.maximum(tau_u ^ INT_MIN, KEY_NEG_INF + 1)

    n_admitted = count(lambda kk, _: kk >= tau)
    has_tie = jnp.max(jnp.where(n_admitted > float(topk), 1.0, 0.0)) > 0.5

    @pl.when(has_tie)
    def _():
        need = float(topk) - count(lambda kk, _: kk > tau)
        nbits = seq_len.bit_length()

        def tie_bit(b, x):
            cand = x | lax.shift_left(jnp.int32(1), nbits - 1 - b)
            below = count(lambda kk, kidx: jnp.where(kk == tau, kidx, cand) < cand)
            return jnp.where(below < need, cand, x)

        x = lax.fori_loop(0, nbits, tie_bit, jnp.zeros((1, qb), jnp.int32))

        def demote(c, carry):
            start = chunk_start(c)
            kk = keys_ref[pl.ds(start, ck), :]
            drop = jnp.where(kk == tau, k_off + start, x) > x
            keys_ref[pl.ds(start, ck), :] = jnp.where(drop, KEY_NEG_INF, kk)
            return carry

        lax.fori_loop(0, n_chunks, demote, 0)

    m_ref[...] = jnp.full(m_ref.shape, NEG_BIG, f32)
    l_ref[...] = jnp.zeros(l_ref.shape, f32)
    acc_ref[...] = jnp.zeros(acc_ref.shape, f32)
    c_exp = (HEAD_DIM ** -0.5) * 1.4426950408889634
    pairs = [slice(hp * 2 * qb, (hp + 1) * 2 * qb) for hp in range(A_HEADS // 2)]

    def attn_chunk(c, carry):
        start = chunk_start(c)
        kc = kn_ref[pl.ds(start, ck), :]
        vt = vt_ref[:, pl.ds(start, ck)]
        sel = keys_ref[pl.ds(start, ck), :] >= tau
        raw = [_dot(kc, qt_ref[:, cols]) for cols in pairs]
        probs, alphas = [], []
        for cols, s in zip(pairs, raw):
            s = jnp.concatenate([jnp.where(sel, s[:, :qb], NEG_BIG), jnp.where(sel, s[:, qb:], NEG_BIG)], axis=1)
            m_old = m_ref[:, cols]
            m_new = jnp.maximum(m_old, jnp.max(s, axis=0, keepdims=True))
            alpha = jnp.exp2((m_old - m_new) * c_exp)
            p = jnp.exp2((s - m_new) * c_exp)
            l_ref[:, cols] = alpha * l_ref[:, cols] + jnp.sum(p, axis=0, keepdims=True)
            m_ref[:, cols] = m_new
            probs.append(p.astype(bf))
            alphas.append(alpha)
        for cols, p, alpha in zip(pairs, probs, alphas):
            acc_ref[:, cols] = alpha * acc_ref[:, cols] + _dot(vt, p)
        return carry

    lax.fori_loop(0, n_chunks, attn_chunk, 0)

    for h in range(A_HEADS):
        cols = slice(h * qb, (h + 1) * qb)
        o_t = acc_ref[:, cols] / l_ref[:, cols]
        o_ref[:, h * HEAD_DIM:(h + 1) * HEAD_DIM] = o_t.T.astype(o_ref.dtype)


def dsa_mixer(proj, kn, vt, ki, q_gain, batch, seq_len, ck=512):
    t = proj.shape[0]
    qb = Q_BLOCK
    nq = seq_len // qb
    ck = min(ck, seq_len)
    topk = min(TOPK_MAX, seq_len // 4)
    w = BRANCH_W
    return pl.pallas_call(
        functools.partial(_dsa_kernel, topk=topk, ck=ck, seq_len=seq_len),
        grid=(batch, nq),
        in_specs=[pl.BlockSpec((qb, w), lambda b, i: (b * nq + i, P_AQ // w)),
                  pl.BlockSpec((qb, w), lambda b, i: (b * nq + i, P_IQ // w)),
                  pl.BlockSpec((qb, HEAD_DIM), lambda b, i: (b * nq + i, P_SM // HEAD_DIM)),
                  pl.BlockSpec((seq_len, HEAD_DIM), lambda b, i: (b, 0)),
                  pl.BlockSpec((HEAD_DIM, seq_len), lambda b, i: (0, b)),
                  pl.BlockSpec((seq_len, IDX_DIM), lambda b, i: (b, 0)),
                  pl.BlockSpec((1, HEAD_DIM), lambda b, i: (0, 0))],
        out_specs=pl.BlockSpec((qb, w), lambda b, i: (b * nq + i, 0)),
        out_shape=jax.ShapeDtypeStruct((t, w), MXU_DTYPE),
        scratch_shapes=[pltpu.VMEM((seq_len, qb), jnp.int32),
                        pltpu.VMEM((HEAD_DIM, A_HEADS * qb), MXU_DTYPE),
                        pltpu.VMEM((IDX_DIM, IDX_HEADS * qb), MXU_DTYPE),
                        pltpu.VMEM((1, A_HEADS * qb), jnp.float32),
                        pltpu.VMEM((1, A_HEADS * qb), jnp.float32),
                        pltpu.VMEM((HEAD_DIM, A_HEADS * qb), jnp.float32)],
        compiler_params=_cparams("parallel", "arbitrary"),
        name="mixer_a",
    )(proj, proj, proj, kn, vt, ki, q_gain.reshape(1, HEAD_DIM))


def _b_prep_kernel(q_ref, k_ref, v_ref, qp_ref, kp_ref, vp_ref, sm_ref, cw_ref, alog_ref, dtb_ref,
                   qo_ref, ko_ref, vo_ref, g_ref, beta_ref, ext_ref, *, tiles_per_seq):
    tl = q_ref.shape[0]
    w = BRANCH_W
    first = (pl.program_id(0) % tiles_per_seq) == 0
    parts = ((q_ref, qp_ref, qo_ref, True), (k_ref, kp_ref, ko_ref, True), (v_ref, vp_ref, vo_ref, False))
    for n, (cur, prev, out, normed) in enumerate(parts):
        ext_ref[0:HALO, :] = jnp.where(first, 0.0, prev[...])
        ext_ref[HALO:HALO + tl, :] = cur[...]
        cw = cw_ref[:, n * w:(n + 1) * w]
        acc = cw[B_CONV - 1:B_CONV, :] * cur[...]
        for j in range(B_CONV - 1):
            off = HALO - (B_CONV - 1) + j
            acc = acc + cw[j:j + 1, :] * ext_ref[off:off + tl, :]
        y = acc * jax.nn.sigmoid(acc)
        if normed:
            for h in range(B_HEADS):
                cols = slice(h * HEAD_DIM, (h + 1) * HEAD_DIM)
                yh = y[:, cols]
                out[:, cols] = yh * lax.rsqrt(jnp.sum(yh * yh, axis=-1, keepdims=True) + EPS)
        else:
            out[...] = y
    a = sm_ref[:, SM_BA:SM_BA + B_HEADS] + dtb_ref[...]
    softplus = jnp.maximum(a, 0.0) + jnp.log(1.0 + jnp.exp(-jnp.abs(a)))
    g_ref[...] = -jnp.exp(alog_ref[...]) * softplus
    beta_ref[...] = jax.nn.sigmoid(sm_ref[:, SM_BB:SM_BB + B_HEADS])


def deltanet_prep(proj, conv_w, a_log, dt_bias, seq_len, tl=256):
    t = proj.shape[0]
    w = BRANCH_W
    cb = P_BQ // w
    cur = lambda c: pl.BlockSpec((tl, w), lambda i: (i, c))
    prev = lambda c: pl.BlockSpec((HALO, w), lambda i: (jnp.maximum(i * (tl // HALO) - 1, 0), c))
    row = lambda width: pl.BlockSpec((tl, width), lambda i: (i, 0))
    return pl.pallas_call(
        functools.partial(_b_prep_kernel, tiles_per_seq=seq_len // tl),
        grid=(t // tl,),
        in_specs=[cur(cb), cur(cb + 1), cur(cb + 2), prev(cb), prev(cb + 1), prev(cb + 2),
                  pl.BlockSpec((tl, HEAD_DIM), lambda i: (i, P_SM // HEAD_DIM)),
                  pl.BlockSpec((B_CONV, 3 * w), lambda i: (0, 0)),
                  pl.BlockSpec((1, B_HEADS), lambda i: (0, 0)),
                  pl.BlockSpec((1, B_HEADS), lambda i: (0, 0))],
        out_specs=[row(w), row(w), row(w), row(B_HEADS), row(B_HEADS)],
        out_shape=[jax.ShapeDtypeStruct((t, w), jnp.float32)] * 3
        + [jax.ShapeDtypeStruct((t, B_HEADS), jnp.float32)] * 2,
        scratch_shapes=[pltpu.VMEM((tl + HALO, w), jnp.float32)],
        compiler_params=_cparams("parallel"),
        name="mixer_b_prep",
    )(proj, proj, proj, proj, proj, proj, proj, conv_w, a_log.reshape(1, B_HEADS), dt_bias.reshape(1, B_HEADS))


def _delta_kernel(q_ref, k_ref, v_ref, g_ref, beta_ref, z_ref, gain_ref, o_ref, state_ref):
    nb = q_ref.shape[0]
    cs = B_CHUNK
    bf = MXU_DTYPE
    f32 = jnp.float32

    @pl.when(pl.program_id(0) == 0)
    def _():
        state_ref[...] = jnp.zeros(state_ref.shape, f32)

    row = lax.broadcasted_iota(jnp.int32, (cs, cs), 0)
    col = lax.broadcasted_iota(jnp.int32, (cs, cs), 1)
    incl = row >= col
    strict = row > col
    eye_c = (row == col).astype(f32)
    eye_h = _eye(B_HEADS, f32)
    eye_d = _eye(HEAD_DIM, bf)
    tril = incl.astype(f32)
    gain = gain_ref[...]
    qscale = HEAD_DIM ** -0.5

    chains = []
    for b in range(nb):
        gc = _dot(tril, g_ref[b], precision=HIGHEST)
        gc_t = _nt_dot(eye_h, gc, precision=HIGHEST)
        eg = jnp.exp(gc)
        g_last = gc[cs - 1:cs, :]
        ek = jnp.exp(g_last - gc)
        e_last = jnp.exp(g_last)
        beta = beta_ref[b]
        for h in range(B_HEADS):
            cols = slice(h * HEAD_DIM, (h + 1) * HEAD_DIM)
            qh = q_ref[b, :, cols] * qscale
            kh = k_ref[b, :, cols]
            bh = beta[:, h:h + 1]
            kb = kh * bh
            decay = jnp.exp(jnp.where(incl, gc[:, h:h + 1] - gc_t[h:h + 1, :], -jnp.inf))
            chains.append(dict(
                b=b, h=h, cols=cols, idx=b * B_HEADS + h, decay=decay,
                q_b=qh.astype(bf), k_b=kh.astype(bf), kb_b=kb.astype(bf),
                vb_b=(v_ref[b, :, cols] * bh).astype(bf),
                kbe_b=(kb * eg[:, h:h + 1]).astype(bf),
                qdec_b=(qh * eg[:, h:h + 1]).astype(bf),
                kdec_b=(kh * ek[:, h:h + 1]).astype(bf),
                e_last=e_last[:, h:h + 1]))

    for ch in chains:
        ch["lower"] = jnp.where(strict, _nt_dot(ch["kb_b"], ch["k_b"]) * ch["decay"], 0.0)
    for ch in chains:
        ch["attn_b"] = jnp.where(incl, _nt_dot(ch["q_b"], ch["k_b"]) * ch["decay"], 0.0).astype(bf)
    for ch in chains:
        ch["inv"] = eye_c - ch["lower"]
        ch["pw"] = ch["lower"]
    for _ in range(5):
        for ch in chains:
            ch["pw"] = _dot3(ch["pw"], ch["pw"])
        for ch in chains:
            ch["inv"] = ch["inv"] + _dot3(ch["inv"], ch["pw"])
    for ch in chains:
        ch["inv_b"] = ch["inv"].astype(bf)
        ch["u"] = _dot(ch["inv_b"], ch["vb_b"])
    for ch in chains:
        ch["w_b"] = _dot(ch["inv_b"], ch["kbe_b"]).astype(bf)
    for ch in chains:
        ch["state_b"] = state_ref[ch["idx"]].astype(bf)
        ch["vnew_b"] = (ch["u"] - _dot(ch["w_b"], ch["state_b"])).astype(bf)
    for ch in chains:
        ch["out"] = _dot(ch["qdec_b"], ch["state_b"]) + _dot(ch["attn_b"], ch["vnew_b"])
    for ch in chains:
        ch["kdec_t"] = _nt_dot(eye_d, ch["kdec_b"]).astype(bf)
    for ch in chains:
        state_ref[ch["idx"]] = state_ref[ch["idx"]] * ch["e_last"] + _dot(ch["kdec_t"], ch["vnew_b"])
    for ch in chains:
        out = ch["out"]
        on = out * lax.rsqrt(jnp.mean(out * out, axis=-1, keepdims=True) + EPS) * gain
        z = z_ref[ch["b"], :, ch["cols"]]
        o_ref[ch["b"], :, ch["cols"]] = (on * (z * jax.nn.sigmoid(z))).astype(o_ref.dtype)


def deltanet_mixer(proj, q, k, v, g, beta, out_gain, batch, seq_len):
    w = BRANCH_W
    cs = B_CHUNK
    r3 = lambda a: a.reshape(batch, seq_len, a.shape[-1])
    blk = lambda width, cb=0: pl.BlockSpec((batch, cs, width), lambda c: (0, c, cb))
    out = pl.pallas_call(
        _delta_kernel,
        grid=(seq_len // cs,),
        in_specs=[blk(w), blk(w), blk(w), blk(B_HEADS), blk(B_HEADS), blk(w, P_BZ // w),
                  pl.BlockSpec((1, HEAD_DIM), lambda c: (0, 0))],
        out_specs=blk(w),
        out_shape=jax.ShapeDtypeStruct((batch, seq_len, w), MXU_DTYPE),
        scratch_shapes=[pltpu.VMEM((batch * B_HEADS, HEAD_DIM, HEAD_DIM), jnp.float32)],
        compiler_params=_cparams("arbitrary"),
        name="mixer_b",
    )(r3(q), r3(k), r3(v), r3(g), r3(beta), r3(proj), out_gain.reshape(1, HEAD_DIM))
    return out.reshape(batch * seq_len, w)


def _merge_kernel(oa_ref, ob_ref, oc_ref, od_ref, g_ref, wb_ref, wg_ref, o_ref):
    lat = g_ref[...].astype(MXU_DTYPE)
    acc = None
    for n, o in enumerate((oa_ref, ob_ref, oc_ref, od_ref)):
        y = _dot(o[...], wb_ref[n])
        gate = jax.nn.sigmoid(_dot(lat, wg_ref[n]))
        acc = gate * y if acc is None else acc + gate * y
    o_ref[...] = acc.astype(o_ref.dtype)


def merge_branches(outs, proj, w_branch, w_gate, layer, tm=1024, tn=512):
    t = proj.shape[0]
    w = BRANCH_W
    tm = min(tm, t)
    n_br = len(outs)
    return pl.pallas_call(
        _merge_kernel,
        grid=(t // tm, D_MODEL // tn),
        in_specs=[pl.BlockSpec((tm, w), lambda i, j: (i, 0))] * n_br
        + [pl.BlockSpec((tm, GATE_RANK), lambda i, j: (i, P_G // GATE_RANK)),
           pl.BlockSpec((None, n_br, w, tn), lambda i, j: (layer, 0, 0, j)),
           pl.BlockSpec((None, n_br, GATE_RANK, tn), lambda i, j: (layer, 0, 0, j))],
        out_specs=pl.BlockSpec((tm, tn), lambda i, j: (i, j)),
        out_shape=jax.ShapeDtypeStruct((t, D_MODEL), MXU_DTYPE),
        compiler_params=_cparams("parallel", "parallel"),
        name="merge",
    )(*outs, proj, w_branch, w_gate)


N_IN = 11872
W_IN_SEGMENTS = (
    (0, 1024, P_AQ),
    (1280, 1024, P_IQ),
    (2384, 3072, P_BQ),
    (5456, 1024, P_BZ),
    (6496, 2048, P_CU),
    (8544, 3072, P_DH),
    (11616, 256, P_G),
    (1024, 128, P_AK),
    (1152, 128, P_AV),
    (2304, 64, P_SM + SM_IK),
    (2368, 16, P_SM + SM_IW),
    (6480, 8, P_SM + SM_BA),
    (6488, 8, P_SM + SM_BB),
)
SM_USED = SM_BB + 8


def _permute_w_in_kernel(w_ref, o_ref):
    for src, width, dst in W_IN_SEGMENTS:
        o_ref[:, dst:dst + width] = w_ref[:, src:src + width].astype(o_ref.dtype)
    o_ref[:, P_SM + SM_USED:N_P] = jnp.zeros((o_ref.shape[0], N_P - P_SM - SM_USED), o_ref.dtype)


def permute_w_in(w, tr=256):
    depth, d, n_in = w.shape
    assert n_in == N_IN
    return pl.pallas_call(
        _permute_w_in_kernel,
        grid=(depth, d // tr),
        in_specs=[pl.BlockSpec((None, tr, N_IN), lambda l, i: (l, i, 0))],
        out_specs=pl.BlockSpec((None, tr, N_P), lambda l, i: (l, i, 0)),
        out_shape=jax.ShapeDtypeStruct((depth, d, N_P), MXU_DTYPE),
        compiler_params=_cparams("parallel", "parallel"),
        name="permute_w_in",
    )(w)


def kernel(x, mix_norm, w_in, a_q_norm, a_k_norm, a_idx_k_norm, b_conv, b_a_log, b_dt_bias, b_out_norm,
           c_ln_gain, c_ln_bias, c_spatial_w, c_spatial_b, d_conv, w_branch, w_gate_up, w_out, ffn_norm,
           w_ff1, w_ff2):
    bn, seq_len, d = x.shape
    depth = w_in.shape[0]
    bf = MXU_DTYPE
    w_in_p = permute_w_in(w_in)
    w_branch_b = w_branch.astype(bf).reshape(depth, 4, BRANCH_W, d)
    w_gate_b = w_gate_up.astype(bf)
    w_out_b = w_out.astype(bf)
    w_ff1_b = w_ff1.astype(bf)
    w_ff2_b = w_ff2.astype(bf)

    xt = x.reshape(bn * seq_len, d)
    xb, ssq = norm_stats(xt, mix_norm[0])
    for l in range(depth):
        proj = matmul(xb, w_in_p, l, row_ssq=ssq)
        kn, vt, ki = dsa_prep(proj, a_k_norm[l], a_idx_k_norm[l])
        o_a = dsa_mixer(proj, kn, vt, ki, a_q_norm[l], bn, seq_len)
        bq, bk, bv, bg, bbeta = deltanet_prep(proj, b_conv[l], b_a_log[l], b_dt_bias[l], seq_len)
        o_b = deltanet_mixer(proj, bq, bk, bv, bg, bbeta, b_out_norm[l], bn, seq_len)
        o_c = spatial_gating_mixer(proj, c_ln_gain[l], c_ln_bias[l], c_spatial_w[l], c_spatial_b[l])
        o_d = short_conv_mixer(proj, d_conv[l], seq_len)
        merged = merge_branches((o_a, o_b, o_c, o_d), proj, w_branch_b, w_gate_b, l)
        xt, xb, ssq = matmul(merged, w_out_b, l, epilogue="residual", residual=xt, next_gain=ffn_norm[l], tn=512)
        mid = matmul(xb, w_ff1_b, l, epilogue="relu2", row_ssq=ssq, out_dtype=bf)
        half = w_ff2_b.shape[1] // 2
        ff2_tiles = dict(tm=1024, tn=256, tk=half)
        xt = matmul(mid, w_ff2_b, l, epilogue="residual", residual=xt, k_range=(0, half), **ff2_tiles)
        if l + 1 < depth:
            xt, xb, ssq = matmul(mid, w_ff2_b, l, epilogue="residual", residual=xt, next_gain=mix_norm[l + 1],
                                 k_range=(half, half), **ff2_tiles)
        else:
            xt = matmul(mid, w_ff2_b, l, epilogue="residual", residual=xt, k_range=(half, half), **ff2_tiles)
    return xt.reshape(bn, seq_len, d)
```

```python
import functools

import jax
import jax.numpy as jnp
from jax import lax
from jax.experimental import pallas as pl
from jax.experimental.pallas import tpu as pltpu

D_MODEL = 4096
HEAD_DIM = 128
BRANCH_W = 1024
A_HEADS = 8
IDX_HEADS = 16
IDX_DIM = 64
IDX_W_SCALE = (IDX_HEADS * IDX_DIM) ** -0.5
TOPK_MAX = 256
Q_BLOCK = 128
B_HEADS = 8
B_CONV = 4
B_CHUNK = 64
C_CHUNK = 128
C_GROUPS = 8
D_CONV = 3
GATE_RANK = 256
EPS = 1e-6

P_AQ, P_IQ, P_BQ, P_BZ, P_CU, P_CV, P_DH, P_DB, P_DC = 0, 1024, 2048, 5120, 6144, 7168, 8192, 9216, 10240
P_G, P_AK, P_AV, P_SM = 11264, 11520, 11648, 11776
N_P = 11904
SM_IK, SM_IW, SM_BA, SM_BB = 0, 64, 80, 88

MXU_DTYPE = jnp.bfloat16
LANES = 128
HALO = 8
VMEM_LIMIT = 56 * 1024 * 1024
NEG_BIG = -1e30
LOG2_E = 1.4426950408889634
INT_MIN = -(2 ** 31)
KEY_NEG_INF = INT_MIN + 0x7FFFFF
HIGHEST = lax.Precision.HIGHEST


def _cparams(*sem):
    return pltpu.CompilerParams(dimension_semantics=sem, vmem_limit_bytes=VMEM_LIMIT)


def _nt_dot(a, b, precision=None):
    return lax.dot_general(a, b, (((1,), (1,)), ((), ())), precision=precision,
                           preferred_element_type=jnp.float32)


def _dot(a, b, precision=None):
    return jnp.dot(a, b, precision=precision, preferred_element_type=jnp.float32)


def _dot3(a, b):
    a_hi, b_hi = a.astype(MXU_DTYPE), b.astype(MXU_DTYPE)
    a_lo = (a - a_hi.astype(jnp.float32)).astype(MXU_DTYPE)
    b_lo = (b - b_hi.astype(jnp.float32)).astype(MXU_DTYPE)
    return _dot(a_hi, b_hi) + (_dot(a_hi, b_lo) + _dot(a_lo, b_hi))


def _eye(n, dtype):
    r = lax.broadcasted_iota(jnp.int32, (n, n), 0)
    c = lax.broadcasted_iota(jnp.int32, (n, n), 1)
    return (r == c).astype(dtype)


def _lane_partial_sq(x):
    sq = x * x
    out = sq[:, 0:LANES]
    for g in range(1, x.shape[1] // LANES):
        out = out + sq[:, g * LANES:(g + 1) * LANES]
    return out


def _norm_stats_kernel(x_ref, g_ref, xb_ref, ssq_ref):
    x = x_ref[...]
    xb_ref[...] = (x * g_ref[...]).astype(xb_ref.dtype)
    ssq_ref[...] = _lane_partial_sq(x)


def norm_stats(x, gain, tl=256):
    t, d = x.shape
    return pl.pallas_call(
        _norm_stats_kernel,
        grid=(t // tl,),
        in_specs=[pl.BlockSpec((tl, d), lambda i: (i, 0)),
                  pl.BlockSpec((1, d), lambda i: (0, 0))],
        out_specs=[pl.BlockSpec((tl, d), lambda i: (i, 0)),
                   pl.BlockSpec((tl, LANES), lambda i: (i, 0))],
        out_shape=[jax.ShapeDtypeStruct((t, d), MXU_DTYPE),
                   jax.ShapeDtypeStruct((t, LANES), jnp.float32)],
        compiler_params=_cparams("parallel"),
        name="norm_stats",
    )(x, gain.reshape(1, d))


def _mm_kernel(*refs, nk, kdim, epilogue, row_scaled, emit_stats):
    refs = list(refs)
    a_ref, w_ref = refs.pop(0), refs.pop(0)
    ssq_in_ref = refs.pop(0) if row_scaled else None
    r_ref = refs.pop(0) if epilogue == "residual" else None
    g_next_ref = refs.pop(0) if emit_stats else None
    o_ref = refs.pop(0)
    xb_ref, ssq_out_ref = (refs.pop(0), refs.pop(0)) if emit_stats else (None, None)
    scratch = refs
    part = _dot(a_ref[...], w_ref[...])

    def finish(acc):
        if row_scaled:
            acc = acc * lax.rsqrt(jnp.sum(ssq_in_ref[...], axis=-1, keepdims=True) * (1.0 / kdim) + EPS)
        if epilogue == "residual":
            out = r_ref[...] + acc
            o_ref[...] = out
            if emit_stats:
                xb_ref[...] = (out * g_next_ref[...]).astype(xb_ref.dtype)
                j = pl.program_id(1)

                @pl.when(j == 0)
                def _():
                    ssq_out_ref[...] = _lane_partial_sq(out)

                @pl.when(j > 0)
                def _():
                    ssq_out_ref[...] += _lane_partial_sq(out)
        elif epilogue == "relu2":
            o_ref[...] = jnp.square(jnp.maximum(acc, 0.0)).astype(o_ref.dtype)
        else:
            o_ref[...] = acc.astype(o_ref.dtype)

    if nk == 1:
        finish(part)
    else:
        acc_ref = scratch[0]
        k = pl.program_id(2)

        @pl.when(k == 0)
        def _():
            acc_ref[...] = part

        @pl.when(k > 0)
        def _():
            acc_ref[...] += part

        @pl.when(k == nk - 1)
        def _():
            finish(acc_ref[...])


def matmul(a, w_stack, layer, *, epilogue="none", residual=None, row_ssq=None, next_gain=None,
           k_range=None, out_dtype=jnp.float32, tm=1024, tn=1024, tk=4096):
    emit_stats = next_gain is not None
    m, kdim = a.shape
    n = w_stack.shape[2]
    k_start, k_size = k_range if k_range is not None else (0, kdim)
    tm, tn, tk = min(tm, m), min(tn, n), min(tk, k_size)
    nk = k_size // tk
    k0 = k_start // tk
    assert k_size % tk == 0 and k_start % tk == 0
    tile = pl.BlockSpec((tm, tn), lambda i, j, k: (i, j))
    stats = pl.BlockSpec((tm, LANES), lambda i, j, k: (i, 0))
    in_specs = [pl.BlockSpec((tm, tk), lambda i, j, k: (i, k0 + k)),
                pl.BlockSpec((None, tk, tn), lambda i, j, k: (layer, k0 + k, j))]
    args = [a, w_stack]
    if row_ssq is not None:
        in_specs.append(stats)
        args.append(row_ssq)
    if epilogue == "residual":
        in_specs.append(tile)
        args.append(residual)
    if emit_stats:
        in_specs.append(pl.BlockSpec((1, tn), lambda i, j, k: (0, j)))
        args.append(next_gain.reshape(1, n))
    out_specs, out_shape = [tile], [jax.ShapeDtypeStruct((m, n), out_dtype)]
    if emit_stats:
        out_specs += [tile, stats]
        out_shape += [jax.ShapeDtypeStruct((m, n), MXU_DTYPE), jax.ShapeDtypeStruct((m, LANES), jnp.float32)]
    scratch = [pltpu.VMEM((tm, tn), jnp.float32)] if nk > 1 else []
    out = pl.pallas_call(
        functools.partial(_mm_kernel, nk=nk, kdim=kdim, epilogue=epilogue, row_scaled=row_ssq is not None,
                          emit_stats=emit_stats),
        grid=(m // tm, pl.cdiv(n, tn), nk),
        in_specs=in_specs,
        out_specs=out_specs,
        out_shape=out_shape,
        scratch_shapes=scratch,
        compiler_params=_cparams("parallel", "arbitrary", "arbitrary"),
        name="matmul_" + epilogue,
    )(*args)
    return out if emit_stats else out[0]


def _dconv_kernel(h_ref, b_ref, c_ref, hp_ref, cp_ref, w_ref, o_ref, ext_ref, *, tiles_per_seq):
    tl = h_ref.shape[0]
    first = (pl.program_id(0) % tiles_per_seq) == 0
    ext_ref[0:HALO, :] = jnp.where(first, 0.0, cp_ref[...] * hp_ref[...])
    p = c_ref[...] * h_ref[...]
    ext_ref[HALO:HALO + tl, :] = p
    w = w_ref[...]
    acc = w[2:3, :] * p
    acc = acc + w[1:2, :] * ext_ref[HALO - 1:HALO - 1 + tl, :]
    acc = acc + w[0:1, :] * ext_ref[HALO - 2:HALO - 2 + tl, :]
    o_ref[...] = (b_ref[...] * acc).astype(o_ref.dtype)


def short_conv_mixer(proj, conv_w, seq_len, tl=512):
    t = proj.shape[0]
    w = BRANCH_W
    cur = lambda cb: pl.BlockSpec((tl, w), lambda i: (i, cb))
    prev = lambda cb: pl.BlockSpec((HALO, w), lambda i: (jnp.maximum(i * (tl // HALO) - 1, 0), cb))
    return pl.pallas_call(
        functools.partial(_dconv_kernel, tiles_per_seq=seq_len // tl),
        grid=(t // tl,),
        in_specs=[cur(P_DH // w), cur(P_DB // w), cur(P_DC // w), prev(P_DH // w), prev(P_DC // w),
                  pl.BlockSpec((D_CONV, w), lambda i: (0, 0))],
        out_specs=pl.BlockSpec((tl, w), lambda i: (i, 0)),
        out_shape=jax.ShapeDtypeStruct((t, w), MXU_DTYPE),
        scratch_shapes=[pltpu.VMEM((tl + HALO, w), jnp.float32)],
        compiler_params=_cparams("parallel"),
        name="mixer_d",
    )(proj, proj, proj, proj, proj, conv_w)


def _gmlp_kernel(u_ref, v_ref, lg_ref, lb_ref, ws_ref, bs_ref, o_ref):
    tl = u_ref.shape[0]
    gd = BRANCH_W // C_GROUPS
    row = lax.broadcasted_iota(jnp.int32, (C_CHUNK, C_CHUNK), 0)
    col = lax.broadcasted_iota(jnp.int32, (C_CHUNK, C_CHUNK), 1)
    causal = row >= col
    for g in range(C_GROUPS):
        w_m = jnp.where(causal, ws_ref[g], 0.0).astype(MXU_DTYPE)
        bias = bs_ref[:, g:g + 1]
        gain = lg_ref[:, g * gd:(g + 1) * gd]
        shift = lb_ref[:, g * gd:(g + 1) * gd]
        for c in range(tl // C_CHUNK):
            rows = slice(c * C_CHUNK, (c + 1) * C_CHUNK)
            cols = slice(g * gd, (g + 1) * gd)
            v = jax.nn.gelu(v_ref[rows, cols])
            mu = jnp.mean(v, axis=-1, keepdims=True)
            vc = v - mu
            var = jnp.mean(vc * vc, axis=-1, keepdims=True)
            vn = vc * lax.rsqrt(var + EPS) * gain + shift
            s = _dot(w_m, vn.astype(MXU_DTYPE)) + bias
            o_ref[rows, cols] = (jax.nn.gelu(u_ref[rows, cols]) * s).astype(o_ref.dtype)


def spatial_gating_mixer(proj, ln_gain, ln_bias, w_s, b_s, tl=256):
    t = proj.shape[0]
    w = BRANCH_W
    return pl.pallas_call(
        _gmlp_kernel,
        grid=(t // tl,),
        in_specs=[pl.BlockSpec((tl, w), lambda i: (i, P_CU // w)),
                  pl.BlockSpec((tl, w), lambda i: (i, P_CV // w)),
                  pl.BlockSpec((1, w), lambda i: (0, 0)),
                  pl.BlockSpec((1, w), lambda i: (0, 0)),
                  pl.BlockSpec((C_GROUPS, C_CHUNK, C_CHUNK), lambda i: (0, 0, 0)),
                  pl.BlockSpec((C_CHUNK, C_GROUPS), lambda i: (0, 0))],
        out_specs=pl.BlockSpec((tl, w), lambda i: (i, 0)),
        out_shape=jax.ShapeDtypeStruct((t, w), MXU_DTYPE),
        compiler_params=_cparams("parallel"),
        name="mixer_c",
    )(proj, proj, ln_gain.reshape(1, w), ln_bias.reshape(1, w), w_s, b_s.T)


def _a_prep_kernel(k_ref, v_ref, sm_ref, gk_ref, gi_ref, kn_ref, vt_ref, ki_ref):
    k = k_ref[...]
    kn = k * lax.rsqrt(jnp.mean(k * k, axis=-1, keepdims=True) + EPS) * gk_ref[...]
    kn_ref[...] = kn.astype(kn_ref.dtype)
    vt_ref[...] = _nt_dot(_eye(HEAD_DIM, MXU_DTYPE), v_ref[...].astype(MXU_DTYPE)).astype(vt_ref.dtype)
    ik = sm_ref[:, SM_IK:SM_IK + IDX_DIM]
    ikn = ik * lax.rsqrt(jnp.mean(ik * ik, axis=-1, keepdims=True) + EPS) * gi_ref[...]
    ki_ref[...] = ikn.astype(ki_ref.dtype)


def dsa_prep(proj, k_gain, ik_gain, tl=512):
    t = proj.shape[0]
    hd = HEAD_DIM
    return pl.pallas_call(
        _a_prep_kernel,
        grid=(t // tl,),
        in_specs=[pl.BlockSpec((tl, hd), lambda i: (i, P_AK // hd)),
                  pl.BlockSpec((tl, hd), lambda i: (i, P_AV // hd)),
                  pl.BlockSpec((tl, hd), lambda i: (i, P_SM // hd)),
                  pl.BlockSpec((1, hd), lambda i: (0, 0)),
                  pl.BlockSpec((1, IDX_DIM), lambda i: (0, 0))],
        out_specs=[pl.BlockSpec((tl, hd), lambda i: (i, 0)),
                   pl.BlockSpec((hd, tl), lambda i: (0, i)),
                   pl.BlockSpec((tl, IDX_DIM), lambda i: (i, 0))],
        out_shape=[jax.ShapeDtypeStruct((t, hd), MXU_DTYPE),
                   jax.ShapeDtypeStruct((hd, t), MXU_DTYPE),
                   jax.ShapeDtypeStruct((t, IDX_DIM), MXU_DTYPE)],
        compiler_params=_cparams("parallel"),
        name="mixer_a_prep",
    )(proj, proj, proj, k_gain.reshape(1, hd), ik_gain.reshape(1, IDX_DIM))


def _dsa_kernel(q_ref, iq_ref, sm_ref, kn_ref, vt_ref, ki_ref, gq_ref, o_ref,
                keys_ref, qt_ref, qit_ref, m_ref, l_ref, acc_ref, *, topk, ck, seq_len):
    qb = Q_BLOCK
    bf = MXU_DTYPE
    f32 = jnp.float32
    i = pl.program_id(1)
    n_chunks = (i * qb + qb + ck - 1) // ck

    def chunk_start(c):
        return pl.multiple_of(c * ck, ck)

    def col_sum(x):
        return jnp.sum(jnp.sum(x.reshape(4, ck // 32, 8, qb), axis=1), axis=0)

    gq = gq_ref[...]
    eye_d = _eye(HEAD_DIM, bf)
    for h in range(A_HEADS):
        qh = q_ref[:, h * HEAD_DIM:(h + 1) * HEAD_DIM]
        qn = qh * lax.rsqrt(jnp.mean(qh * qh, axis=-1, keepdims=True) + EPS) * gq
        qt_ref[:, h * qb:(h + 1) * qb] = _nt_dot(eye_d, qn.astype(bf)).astype(bf)
    eye_i = _eye(IDX_DIM, bf)
    for h in range(IDX_HEADS):
        qi = iq_ref[:, h * IDX_DIM:(h + 1) * IDX_DIM].astype(bf)
        qit_ref[:, h * qb:(h + 1) * qb] = _nt_dot(eye_i, qi).astype(bf)
    w_t = _nt_dot(_eye(IDX_HEADS, f32), sm_ref[:, SM_IW:SM_IW + IDX_HEADS] * IDX_W_SCALE,
                  precision=HIGHEST)

    q_pos = i * qb + lax.broadcasted_iota(jnp.int32, (ck, qb), 1)
    k_off = lax.broadcasted_iota(jnp.int32, (ck, qb), 0)

    def score_chunk(c, carry):
        start = chunk_start(c)
        kc = ki_ref[pl.ds(start, ck), :]
        acc = jnp.zeros((ck, qb), f32)
        for hp in range(IDX_HEADS // 2):
            r = _dot(kc, qit_ref[:, hp * 2 * qb:(hp + 1) * 2 * qb])
            for s in range(2):
                h = 2 * hp + s
                acc = acc + w_t[h:h + 1, :] * jnp.maximum(r[:, s * qb:(s + 1) * qb], 0.0)
        score = jnp.where(k_off + start <= q_pos, acc, -jnp.inf)
        bits = pltpu.bitcast(score, jnp.int32)
        keys_ref[pl.ds(start, ck), :] = bits ^ ((bits >> 31) & 0x7FFFFFFF)
        return carry

    lax.fori_loop(0, n_chunks, score_chunk, 0)

    def count(pred):
        def hits(c):
            start = chunk_start(c)
            return col_sum(jnp.where(pred(keys_ref[pl.ds(start, ck), :], k_off + start), 1.0, 0.0))

        def two_chunks(j, cnt):
            second = jnp.minimum(2 * j + 1, n_chunks - 1)
            valid = (2 * j + 1 < n_chunks).astype(f32)
            return cnt + hits(2 * j) + hits(second) * valid

        cnt = lax.fori_loop(0, (n_chunks + 1) // 2, two_chunks, jnp.zeros((8, qb), f32))
        return jnp.sum(cnt, axis=0, keepdims=True)

    def bit_step(b, tau_u):
        cand_u = tau_u | lax.shift_left(jnp.int32(1), 31 - b)
        cand_s = cand_u ^ INT_MIN
        total = count(lambda kk, _: kk >= cand_s)
        return jnp.where(total >= float(topk), cand_u, tau_u)

    tau_u = lax.fori_loop(0, 32, bit_step, jnp.zeros((1, qb), jnp.int32))
    tau = jnp.maximum(tau_u ^ INT_MIN, KEY_NEG_INF + 1)

    n_admitted = count(lambda kk, _: kk >= tau)
    has_tie = jnp.max(jnp.where(n_admitted > float(topk), 1.0, 0.0)) > 0.5

    @pl.when(has_tie)
    def _():
        need = float(topk) - count(lambda kk, _: kk > tau)
        nbits = seq_len.bit_length()

        def tie_bit(b, x):
            cand = x | lax.shift_left(jnp.int32(1), nbits - 1 - b)
            below = count(lambda kk, kidx: jnp.where(kk == tau, kidx, cand) < cand)
            return jnp.where(below < need, cand, x)

        x = lax.fori_loop(0, nbits, tie_bit, jnp.zeros((1, qb), jnp.int32))

        def demote(c, carry):
            start = chunk_start(c)
            kk = keys_ref[pl.ds(start, ck), :]
            drop = jnp.where(kk == tau, k_off + start, x) > x
            keys_ref[pl.ds(start, ck), :] = jnp.where(drop, KEY_NEG_INF, kk)
            return carry

        lax.fori_loop(0, n_chunks, demote, 0)

    m_ref[...] = jnp.full(m_ref.shape, NEG_BIG, f32)
    l_ref[...] = jnp.zeros(l_ref.shape, f32)
    acc_ref[...] = jnp.zeros(acc_ref.shape, f32)
    c_exp = (HEAD_DIM ** -0.5) * LOG2_E
    pairs = [slice(hp * 2 * qb, (hp + 1) * 2 * qb) for hp in range(A_HEADS // 2)]

    def attn_chunk(c, carry):
        start = chunk_start(c)
        kc = kn_ref[pl.ds(start, ck), :]
        vt = vt_ref[:, pl.ds(start, ck)]
        sel = keys_ref[pl.ds(start, ck), :] >= tau
        raw = [_dot(kc, qt_ref[:, cols]) for cols in pairs]
        probs, alphas = [], []
        for cols, s in zip(pairs, raw):
            s = jnp.concatenate([jnp.where(sel, s[:, :qb], NEG_BIG), jnp.where(sel, s[:, qb:], NEG_BIG)], axis=1)
            m_old = m_ref[:, cols]
            m_new = jnp.maximum(m_old, jnp.max(s, axis=0, keepdims=True))
            alpha = jnp.exp2((m_old - m_new) * c_exp)
            p = jnp.exp2((s - m_new) * c_exp)
            l_ref[:, cols] = alpha * l_ref[:, cols] + jnp.sum(p, axis=0, keepdims=True)
            m_ref[:, cols] = m_new
            probs.append(p.astype(bf))
            alphas.append(alpha)
        for cols, p, alpha in zip(pairs, probs, alphas):
            acc_ref[:, cols] = alpha * acc_ref[:, cols] + _dot(vt, p)
        return carry

    lax.fori_loop(0, n_chunks, attn_chunk, 0)

    for h in range(A_HEADS):
        cols = slice(h * qb, (h + 1) * qb)
        o_t = acc_ref[:, cols] / l_ref[:, cols]
        o_ref[:, h * HEAD_DIM:(h + 1) * HEAD_DIM] = o_t.T.astype(o_ref.dtype)


def dsa_mixer(proj, kn, vt, ki, q_gain, batch, seq_len, ck=512):
    t = proj.shape[0]
    qb = Q_BLOCK
    nq = seq_len // qb
    ck = min(ck, seq_len)
    topk = min(TOPK_MAX, seq_len // 4)
    w = BRANCH_W
    return pl.pallas_call(
        functools.partial(_dsa_kernel, topk=topk, ck=ck, seq_len=seq_len),
        grid=(batch, nq),
        in_specs=[pl.BlockSpec((qb, w), lambda b, i: (b * nq + i, P_AQ // w)),
                  pl.BlockSpec((qb, w), lambda b, i: (b * nq + i, P_IQ // w)),
                  pl.BlockSpec((qb, HEAD_DIM), lambda b, i: (b * nq + i, P_SM // HEAD_DIM)),
                  pl.BlockSpec((seq_len, HEAD_DIM), lambda b, i: (b, 0)),
                  pl.BlockSpec((HEAD_DIM, seq_len), lambda b, i: (0, b)),
                  pl.BlockSpec((seq_len, IDX_DIM), lambda b, i: (b, 0)),
                  pl.BlockSpec((1, HEAD_DIM), lambda b, i: (0, 0))],
        out_specs=pl.BlockSpec((qb, w), lambda b, i: (b * nq + i, 0)),
        out_shape=jax.ShapeDtypeStruct((t, w), MXU_DTYPE),
        scratch_shapes=[pltpu.VMEM((seq_len, qb), jnp.int32),
                        pltpu.VMEM((HEAD_DIM, A_HEADS * qb), MXU_DTYPE),
                        pltpu.VMEM((IDX_DIM, IDX_HEADS * qb), MXU_DTYPE),
                        pltpu.VMEM((1, A_HEADS * qb), jnp.float32),
                        pltpu.VMEM((1, A_HEADS * qb), jnp.float32),
                        pltpu.VMEM((HEAD_DIM, A_HEADS * qb), jnp.float32)],
        compiler_params=_cparams("parallel", "arbitrary"),
        name="mixer_a",
    )(proj, proj, proj, kn, vt, ki, q_gain.reshape(1, HEAD_DIM))


def _b_prep_kernel(q_ref, k_ref, v_ref, qp_ref, kp_ref, vp_ref, sm_ref, cw_ref, alog_ref, dtb_ref,
                   qo_ref, ko_ref, vo_ref, g_ref, beta_ref, ext_ref, *, tiles_per_seq):
    tl = q_ref.shape[0]
    w = BRANCH_W
    first = (pl.program_id(0) % tiles_per_seq) == 0
    parts = ((q_ref, qp_ref, qo_ref, True), (k_ref, kp_ref, ko_ref, True), (v_ref, vp_ref, vo_ref, False))
    for n, (cur, prev, out, normed) in enumerate(parts):
        ext_ref[0:HALO, :] = jnp.where(first, 0.0, prev[...])
        ext_ref[HALO:HALO + tl, :] = cur[...]
        cw = cw_ref[:, n * w:(n + 1) * w]
        acc = cw[B_CONV - 1:B_CONV, :] * cur[...]
        for j in range(B_CONV - 1):
            off = HALO - (B_CONV - 1) + j
            acc = acc + cw[j:j + 1, :] * ext_ref[off:off + tl, :]
        y = acc * jax.nn.sigmoid(acc)
        if normed:
            for h in range(B_HEADS):
                cols = slice(h * HEAD_DIM, (h + 1) * HEAD_DIM)
                yh = y[:, cols]
                out[:, cols] = yh * lax.rsqrt(jnp.sum(yh * yh, axis=-1, keepdims=True) + EPS)
        else:
            out[...] = y
    a = sm_ref[:, SM_BA:SM_BA + B_HEADS] + dtb_ref[...]
    softplus = jnp.maximum(a, 0.0) + jnp.log(1.0 + jnp.exp(-jnp.abs(a)))
    g_ref[...] = -jnp.exp(alog_ref[...]) * softplus
    beta_ref[...] = jax.nn.sigmoid(sm_ref[:, SM_BB:SM_BB + B_HEADS])


def deltanet_prep(proj, conv_w, a_log, dt_bias, seq_len, tl=256):
    t = proj.shape[0]
    w = BRANCH_W
    cb = P_BQ // w
    cur = lambda c: pl.BlockSpec((tl, w), lambda i: (i, c))
    prev = lambda c: pl.BlockSpec((HALO, w), lambda i: (jnp.maximum(i * (tl // HALO) - 1, 0), c))
    row = lambda width: pl.BlockSpec((tl, width), lambda i: (i, 0))
    return pl.pallas_call(
        functools.partial(_b_prep_kernel, tiles_per_seq=seq_len // tl),
        grid=(t // tl,),
        in_specs=[cur(cb), cur(cb + 1), cur(cb + 2), prev(cb), prev(cb + 1), prev(cb + 2),
                  pl.BlockSpec((tl, HEAD_DIM), lambda i: (i, P_SM // HEAD_DIM)),
                  pl.BlockSpec((B_CONV, 3 * w), lambda i: (0, 0)),
                  pl.BlockSpec((1, B_HEADS), lambda i: (0, 0)),
                  pl.BlockSpec((1, B_HEADS), lambda i: (0, 0))],
        out_specs=[row(w), row(w), row(w), row(B_HEADS), row(B_HEADS)],
        out_shape=[jax.ShapeDtypeStruct((t, w), jnp.float32)] * 3
        + [jax.ShapeDtypeStruct((t, B_HEADS), jnp.float32)] * 2,
        scratch_shapes=[pltpu.VMEM((tl + HALO, w), jnp.float32)],
        compiler_params=_cparams("parallel"),
        name="mixer_b_prep",
    )(proj, proj, proj, proj, proj, proj, proj, conv_w, a_log.reshape(1, B_HEADS), dt_bias.reshape(1, B_HEADS))


def _delta_kernel(q_ref, k_ref, v_ref, g_ref, beta_ref, z_ref, gain_ref, o_ref, state_ref):
    nb = q_ref.shape[0]
    cs = B_CHUNK
    bf = MXU_DTYPE
    f32 = jnp.float32

    @pl.when(pl.program_id(0) == 0)
    def _():
        state_ref[...] = jnp.zeros(state_ref.shape, f32)

    row = lax.broadcasted_iota(jnp.int32, (cs, cs), 0)
    col = lax.broadcasted_iota(jnp.int32, (cs, cs), 1)
    incl = row >= col
    strict = row > col
    eye_c = (row == col).astype(f32)
    eye_h = _eye(B_HEADS, f32)
    eye_d = _eye(HEAD_DIM, bf)
    tril = incl.astype(f32)
    gain = gain_ref[...]
    qscale = HEAD_DIM ** -0.5

    chains = []
    for b in range(nb):
        gc = _dot(tril, g_ref[b], precision=HIGHEST)
        gc_t = _nt_dot(eye_h, gc, precision=HIGHEST)
        eg = jnp.exp(gc)
        g_last = gc[cs - 1:cs, :]
        ek = jnp.exp(g_last - gc)
        e_last = jnp.exp(g_last)
        beta = beta_ref[b]
        for h in range(B_HEADS):
            cols = slice(h * HEAD_DIM, (h + 1) * HEAD_DIM)
            qh = q_ref[b, :, cols] * qscale
            kh = k_ref[b, :, cols]
            bh = beta[:, h:h + 1]
            kb = kh * bh
            decay = jnp.exp(jnp.where(incl, gc[:, h:h + 1] - gc_t[h:h + 1, :], -jnp.inf))
            chains.append(dict(
                b=b, h=h, cols=cols, idx=b * B_HEADS + h, decay=decay,
                q_b=qh.astype(bf), k_b=kh.astype(bf), kb_b=kb.astype(bf),
                vb_b=(v_ref[b, :, cols] * bh).astype(bf),
                kbe_b=(kb * eg[:, h:h + 1]).astype(bf),
                qdec_b=(qh * eg[:, h:h + 1]).astype(bf),
                kdec_b=(kh * ek[:, h:h + 1]).astype(bf),
                e_last=e_last[:, h:h + 1]))

    for ch in chains:
        ch["lower"] = jnp.where(strict, _nt_dot(ch["kb_b"], ch["k_b"]) * ch["decay"], 0.0)
    for ch in chains:
        ch["attn_b"] = jnp.where(incl, _nt_dot(ch["q_b"], ch["k_b"]) * ch["decay"], 0.0).astype(bf)
    for ch in chains:
        ch["inv"] = eye_c - ch["lower"]
        ch["pw"] = ch["lower"]
    for _ in range(5):
        for ch in chains:
            ch["pw"] = _dot3(ch["pw"], ch["pw"])
        for ch in chains:
            ch["inv"] = ch["inv"] + _dot3(ch["inv"], ch["pw"])
    for ch in chains:
        ch["inv_b"] = ch["inv"].astype(bf)
        ch["u"] = _dot(ch["inv_b"], ch["vb_b"])
    for ch in chains:
        ch["w_b"] = _dot(ch["inv_b"], ch["kbe_b"]).astype(bf)
    for ch in chains:
        ch["state_b"] = state_ref[ch["idx"]].astype(bf)
        ch["vnew_b"] = (ch["u"] - _dot(ch["w_b"], ch["state_b"])).astype(bf)
    for ch in chains:
        ch["out"] = _dot(ch["qdec_b"], ch["state_b"]) + _dot(ch["attn_b"], ch["vnew_b"])
    for ch in chains:
        ch["kdec_t"] = _nt_dot(eye_d, ch["kdec_b"]).astype(bf)
    for ch in chains:
        state_ref[ch["idx"]] = state_ref[ch["idx"]] * ch["e_last"] + _dot(ch["kdec_t"], ch["vnew_b"])
    for ch in chains:
        out = ch["out"]
        on = out * lax.rsqrt(jnp.mean(out * out, axis=-1, keepdims=True) + EPS) * gain
        z = z_ref[ch["b"], :, ch["cols"]]
        o_ref[ch["b"], :, ch["cols"]] = (on * (z * jax.nn.sigmoid(z))).astype(o_ref.dtype)


def deltanet_mixer(proj, q, k, v, g, beta, out_gain, batch, seq_len):
    w = BRANCH_W
    cs = B_CHUNK
    r3 = lambda a: a.reshape(batch, seq_len, a.shape[-1])
    blk = lambda width, cb=0: pl.BlockSpec((batch, cs, width), lambda c: (0, c, cb))
    out = pl.pallas_call(
        _delta_kernel,
        grid=(seq_len // cs,),
        in_specs=[blk(w), blk(w), blk(w), blk(B_HEADS), blk(B_HEADS), blk(w, P_BZ // w),
                  pl.BlockSpec((1, HEAD_DIM), lambda c: (0, 0))],
        out_specs=blk(w),
        out_shape=jax.ShapeDtypeStruct((batch, seq_len, w), MXU_DTYPE),
        scratch_shapes=[pltpu.VMEM((batch * B_HEADS, HEAD_DIM, HEAD_DIM), jnp.float32)],
        compiler_params=_cparams("arbitrary"),
        name="mixer_b",
    )(r3(q), r3(k), r3(v), r3(g), r3(beta), r3(proj), out_gain.reshape(1, HEAD_DIM))
    return out.reshape(batch * seq_len, w)


def _merge_kernel(oa_ref, ob_ref, oc_ref, od_ref, g_ref, wb_ref, wg_ref, o_ref):
    lat = g_ref[...].astype(MXU_DTYPE)
    acc = None
    for n, o in enumerate((oa_ref, ob_ref, oc_ref, od_ref)):
        y = _dot(o[...], wb_ref[n])
        gate = jax.nn.sigmoid(_dot(lat, wg_ref[n]))
        acc = gate * y if acc is None else acc + gate * y
    o_ref[...] = acc.astype(o_ref.dtype)


def merge_branches(outs, proj, w_branch, w_gate, layer, tm=1024, tn=1024):
    t = proj.shape[0]
    w = BRANCH_W
    tm = min(tm, t)
    n_br = len(outs)
    return pl.pallas_call(
        _merge_kernel,
        grid=(t // tm, D_MODEL // tn),
        in_specs=[pl.BlockSpec((tm, w), lambda i, j: (i, 0))] * n_br
        + [pl.BlockSpec((tm, GATE_RANK), lambda i, j: (i, P_G // GATE_RANK)),
           pl.BlockSpec((None, n_br, w, tn), lambda i, j: (layer, 0, 0, j)),
           pl.BlockSpec((None, n_br, GATE_RANK, tn), lambda i, j: (layer, 0, 0, j))],
        out_specs=pl.BlockSpec((tm, tn), lambda i, j: (i, j)),
        out_shape=jax.ShapeDtypeStruct((t, D_MODEL), MXU_DTYPE),
        compiler_params=_cparams("parallel", "parallel"),
        name="merge",
    )(*outs, proj, w_branch, w_gate)


N_IN = 11872
W_IN_SEGMENTS = (
    (0, 1024, P_AQ),
    (1280, 1024, P_IQ),
    (2384, 3072, P_BQ),
    (5456, 1024, P_BZ),
    (6496, 2048, P_CU),
    (8544, 3072, P_DH),
    (11616, 256, P_G),
    (1024, 128, P_AK),
    (1152, 128, P_AV),
    (2304, 64, P_SM + SM_IK),
    (2368, 16, P_SM + SM_IW),
    (6480, 8, P_SM + SM_BA),
    (6488, 8, P_SM + SM_BB),
)
SM_USED = SM_BB + 8


def _permute_w_in_kernel(w_ref, o_ref):
    for src, width, dst in W_IN_SEGMENTS:
        o_ref[:, dst:dst + width] = w_ref[:, src:src + width].astype(o_ref.dtype)
    o_ref[:, P_SM + SM_USED:N_P] = jnp.zeros((o_ref.shape[0], N_P - P_SM - SM_USED), o_ref.dtype)


def permute_w_in(w, tr=256):
    depth, d, n_in = w.shape
    assert n_in == N_IN
    return pl.pallas_call(
        _permute_w_in_kernel,
        grid=(depth, d // tr),
        in_specs=[pl.BlockSpec((None, tr, N_IN), lambda l, i: (l, i, 0))],
        out_specs=pl.BlockSpec((None, tr, N_P), lambda l, i: (l, i, 0)),
        out_shape=jax.ShapeDtypeStruct((depth, d, N_P), MXU_DTYPE),
        compiler_params=_cparams("parallel", "parallel"),
        name="permute_w_in",
    )(w)


def kernel(x, mix_norm, w_in, a_q_norm, a_k_norm, a_idx_k_norm, b_conv, b_a_log, b_dt_bias, b_out_norm,
           c_ln_gain, c_ln_bias, c_spatial_w, c_spatial_b, d_conv, w_branch, w_gate_up, w_out, ffn_norm,
           w_ff1, w_ff2):
    bn, seq_len, d = x.shape
    depth = w_in.shape[0]
    bf = MXU_DTYPE
    w_in_p = permute_w_in(w_in)
    w_branch_b = w_branch.astype(bf).reshape(depth, 4, BRANCH_W, d)
    w_gate_b = w_gate_up.astype(bf)
    w_out_b = w_out.astype(bf)
    w_ff1_b = w_ff1.astype(bf)
    w_ff2_b = w_ff2.astype(bf)

    xt = x.reshape(bn * seq_len, d)
    xb, ssq = norm_stats(xt, mix_norm[0])
    for l in range(depth):
        proj = matmul(xb, w_in_p, l, row_ssq=ssq)
        kn, vt, ki = dsa_prep(proj, a_k_norm[l], a_idx_k_norm[l])
        o_a = dsa_mixer(proj, kn, vt, ki, a_q_norm[l], bn, seq_len)
        bq, bk, bv, bg, bbeta = deltanet_prep(proj, b_conv[l], b_a_log[l], b_dt_bias[l], seq_len)
        o_b = deltanet_mixer(proj, bq, bk, bv, bg, bbeta, b_out_norm[l], bn, seq_len)
        o_c = spatial_gating_mixer(proj, c_ln_gain[l], c_ln_bias[l], c_spatial_w[l], c_spatial_b[l])
        o_d = short_conv_mixer(proj, d_conv[l], seq_len)
        merged = merge_branches((o_a, o_b, o_c, o_d), proj, w_branch_b, w_gate_b, l)
        xt, xb, ssq = matmul(merged, w_out_b, l, epilogue="residual", residual=xt, next_gain=ffn_norm[l], tn=512)
        mid = matmul(xb, w_ff1_b, l, epilogue="relu2", row_ssq=ssq, out_dtype=bf)
        half = w_ff2_b.shape[1] // 2
        ff2_tiles = dict(tm=1024, tn=256, tk=half)
        xt = matmul(mid, w_ff2_b, l, epilogue="residual", residual=xt, k_range=(0, half), **ff2_tiles)
        if l + 1 < depth:
            xt, xb, ssq = matmul(mid, w_ff2_b, l, epilogue="residual", residual=xt, next_gain=mix_norm[l + 1],
                                 k_range=(half, half), **ff2_tiles)
        else:
            xt = matmul(mid, w_ff2_b, l, epilogue="residual", residual=xt, k_range=(half, half), **ff2_tiles)
    return xt.reshape(bn, seq_len, d)
```
